```python
import math
import jax, jax.numpy as jnp
from jax import lax
import numpy as np

D_MODEL = 2048
BATCH = 4
SEQ = 4096
DEPTH = 4

BLOCK_Q = 128
EPS = 1e-6
DIFF_HEADS = 8
DIFF_QK_DIM = 64
DIFF_V_DIM = 128
SB_HEADS = 8
SB_HEAD_DIM = 128
MLA_HEADS = 8
MLA_Q_RANK = 512
MLA_KV_RANK = 256
MLA_NOPE_DIM = 128
MLA_ROPE_DIM = 64
MLA_V_DIM = 128
ROPE_THETA = 10000.0
REL_BUCKETS = 32
REL_MAX_DIST = 128
N_GROUPS = 8
EXPERTS_PER_GROUP = 4
N_EXPERTS = N_GROUPS * EXPERTS_PER_GROUP
TOP_K = 2
D_EXPERT = 512
MOE_CHUNK = 512
N_BRANCH = 3
BRANCH_WIDTH = 1024
IN_SIZES = (
    DIFF_HEADS * 2 * DIFF_QK_DIM,
    DIFF_HEADS * 2 * DIFF_QK_DIM,
    DIFF_HEADS * DIFF_V_DIM,
    SB_HEADS * SB_HEAD_DIM,
    SB_HEADS * SB_HEAD_DIM,
    SB_HEADS * SB_HEAD_DIM,
    MLA_Q_RANK,
    MLA_KV_RANK,
    MLA_ROPE_DIM,
    N_BRANCH * D_MODEL,
)
N_IN = sum(IN_SIZES)

kernel_name = "hybrid_diff_sb_mla_hiermoe"


def _rms(x, g):
    x32 = x.astype(jnp.float32)
    y = x32 * lax.rsqrt(jnp.mean(x32 * x32, axis=-1, keepdims=True) + EPS)
    return (y * g.astype(jnp.float32)).astype(x.dtype)


def _rope(x, cos, sin):
    half = x.shape[-1] // 2
    x1, x2 = x[..., :half], x[..., half:]
    return jnp.concatenate([x1 * cos - x2 * sin, x1 * sin + x2 * cos], axis=-1)


def _t5_bucket(n):
    n = jnp.maximum(n, 0)
    max_exact = REL_BUCKETS // 2
    nf = jnp.maximum(n, 1).astype(jnp.float32)
    large = max_exact + (jnp.log(nf / max_exact) / math.log(REL_MAX_DIST / max_exact)
                         * (REL_BUCKETS - max_exact)).astype(jnp.int32)
    large = jnp.minimum(large, REL_BUCKETS - 1)
    return jnp.where(n < max_exact, n, large)


def _block_sweep(block_fn, seq):
    out = lax.map(block_fn, jnp.arange(seq // BLOCK_Q))
    nb, b, h, blk, dv = out.shape
    return out.transpose(1, 2, 0, 3, 4).reshape(b, h, nb * blk, dv)


def _diff_attention(q1, q2, k1, k2, v, rel_bias, lam):
    seq = q1.shape[2]
    scale = DIFF_QK_DIM ** -0.5
    key_pos = jnp.arange(seq)

    def block(i):
        start = i * BLOCK_Q
        rel = (start + jnp.arange(BLOCK_Q))[:, None] - key_pos[None, :]
        causal = rel >= 0
        bias = jnp.transpose(rel_bias[_t5_bucket(rel)], (2, 0, 1)).astype(jnp.float32)

        def probs(q, k):
            qb = lax.dynamic_slice_in_dim(q, start, BLOCK_Q, axis=2)
            s = jnp.einsum('bhqd,bhkd->bhqk', qb, k).astype(jnp.float32) * scale + bias
            return jax.nn.softmax(jnp.where(causal, s, -jnp.inf), axis=-1)

        p = probs(q1, k1) - lam * probs(q2, k2)
        return jnp.einsum('bhqk,bhkd->bhqd', p.astype(v.dtype), v)

    return _block_sweep(block, seq)


def _stick_breaking(q, k, v):
    seq = q.shape[2]
    scale = SB_HEAD_DIM ** -0.5
    key_pos = jnp.arange(seq)

    def block(i):
        start = i * BLOCK_Q
        rel = (start + jnp.arange(BLOCK_Q))[:, None] - key_pos[None, :]
        strict = rel > 0
        qb = lax.dynamic_slice_in_dim(q, start, BLOCK_Q, axis=2)
        z = jnp.einsum('bhqd,bhkd->bhqk', qb, k).astype(jnp.float32) * scale
        log_fail = jnp.where(strict, jax.nn.log_sigmoid(-z), 0.0)
        log_a = z + lax.cumsum(log_fail, axis=3, reverse=True)
        a = jnp.exp(jnp.where(strict, log_a, -jnp.inf))
        return jnp.einsum('bhqk,bhkd->bhqd', a.astype(v.dtype), v)

    return _block_sweep(block, seq)


def _causal_attention(q, k, v, scale):
    seq = q.shape[2]
    key_pos = jnp.arange(seq)

    def block(i):
        start = i * BLOCK_Q
        rel = (start + jnp.arange(BLOCK_Q))[:, None] - key_pos[None, :]
        qb = lax.dynamic_slice_in_dim(q, start, BLOCK_Q, axis=2)
        s = jnp.einsum('bhqd,bhkd->bhqk', qb, k).astype(jnp.float32) * scale
        p = jax.nn.softmax(jnp.where(rel >= 0, s, -jnp.inf), axis=-1)
        return jnp.einsum('bhqk,bhkd->bhqd', p.astype(v.dtype), v)

    return _block_sweep(block, seq)


def _mixer(h, layer_idx, rel_bias, cos, sin, w_in, diff_q_norm, diff_k_norm, diff_lambda,
           diff_subln_g, mla_q_a_norm, mla_w_uq, mla_kv_a_norm, mla_w_ukv, mla_q_norm,
           mla_k_norm, w_branch, b_gate, w_out):
    B, S, _ = h.shape
    proj = h @ w_in
    offsets = np.cumsum(IN_SIZES)[:-1].tolist()
    (dq, dk, dv, sq, sk, sv, cq, ckv, kr, gate_pre) = jnp.split(proj, offsets, axis=-1)

    def heads(t, n, d):
        return t.reshape(B, S, n, d).transpose(0, 2, 1, 3)

    dq = dq.reshape(B, S, DIFF_HEADS, 2, DIFF_QK_DIM)
    dk = dk.reshape(B, S, DIFF_HEADS, 2, DIFF_QK_DIM)
    q1 = _rms(dq[:, :, :, 0], diff_q_norm).transpose(0, 2, 1, 3)
    q2 = _rms(dq[:, :, :, 1], diff_q_norm).transpose(0, 2, 1, 3)
    k1 = _rms(dk[:, :, :, 0], diff_k_norm).transpose(0, 2, 1, 3)
    k2 = _rms(dk[:, :, :, 1], diff_k_norm).transpose(0, 2, 1, 3)
    lam_init = 0.8 - 0.6 * math.exp(-0.3 * layer_idx)
    lp = diff_lambda.astype(jnp.float32)
    lam = jnp.exp(jnp.sum(lp[0] * lp[1])) - jnp.exp(jnp.sum(lp[2] * lp[3])) + lam_init
    o_a = _diff_attention(q1, q2, k1, k2, heads(dv, DIFF_HEADS, DIFF_V_DIM), rel_bias, lam)
    o_a = _rms(o_a, diff_subln_g) * (1.0 - lam_init)
    o_a = o_a.transpose(0, 2, 1, 3).reshape(B, S, DIFF_HEADS * DIFF_V_DIM)

    o_b = _stick_breaking(heads(sq, SB_HEADS, SB_HEAD_DIM), heads(sk, SB_HEADS, SB_HEAD_DIM),
                          heads(sv, SB_HEADS, SB_HEAD_DIM))
    o_b = o_b.transpose(0, 2, 1, 3).reshape(B, S, SB_HEADS * SB_HEAD_DIM)

    q = (_rms(cq, mla_q_a_norm) @ mla_w_uq).reshape(B, S, MLA_HEADS, MLA_NOPE_DIM + MLA_ROPE_DIM)
    q_nope, q_pe = q[..., :MLA_NOPE_DIM], q[..., MLA_NOPE_DIM:]
    q_pe = _rope(q_pe, cos[:, None, :], sin[:, None, :])
    kv = (_rms(ckv, mla_kv_a_norm) @ mla_w_ukv).reshape(B, S, MLA_HEADS, MLA_NOPE_DIM + MLA_V_DIM)
    k_nope, v_c = kv[..., :MLA_NOPE_DIM], kv[..., MLA_NOPE_DIM:]
    k_pe = jnp.broadcast_to(_rope(kr, cos, sin)[:, :, None, :], (B, S, MLA_HEADS, MLA_ROPE_DIM))
    q_c = _rms(jnp.concatenate([q_nope, q_pe], axis=-1), mla_q_norm).transpose(0, 2, 1, 3)
    k_c = _rms(jnp.concatenate([k_nope, k_pe], axis=-1), mla_k_norm).transpose(0, 2, 1, 3)
    o_c = _causal_attention(q_c, k_c, v_c.transpose(0, 2, 1, 3), (MLA_NOPE_DIM + MLA_ROPE_DIM) ** -0.5)
    o_c = o_c.transpose(0, 2, 1, 3).reshape(B, S, MLA_HEADS * MLA_V_DIM)

    gate = jax.nn.sigmoid(gate_pre + b_gate).reshape(B, S, N_BRANCH, D_MODEL)
    merged = (gate[:, :, 0] * (o_a @ w_branch[0])
              + gate[:, :, 1] * (o_b @ w_branch[1])
              + gate[:, :, 2] * (o_c @ w_branch[2]))
    return merged @ w_out


def _hier_moe(h, rg_w, rg_b, re_w, re_b, w_gu, w_down):
    B, S, D = h.shape
    t = h.reshape(-1, D)
    T = t.shape[0]
    g_prob = jax.nn.softmax((t @ rg_w).astype(jnp.float32) + rg_b, axis=-1)
    g_val, g_idx = lax.top_k(g_prob, 1)
    e_logits = ((t @ re_w).astype(jnp.float32) + re_b).reshape(T, N_GROUPS, EXPERTS_PER_GROUP)
    e_in_group = jnp.take_along_axis(e_logits, g_idx[:, :, None], axis=1)[:, 0]
    e_val, e_idx = lax.top_k(e_in_group, TOP_K)
    w = jax.nn.softmax(e_val, axis=-1) * g_val
    expert_id = g_idx * EXPERTS_PER_GROUP + e_idx
    combine = jnp.sum(jax.nn.one_hot(expert_id, N_EXPERTS, dtype=jnp.float32) * w[..., None], axis=1)
    n_chunks = -(-T // MOE_CHUNK)
    pad = n_chunks * MOE_CHUNK - T
    t_p = jnp.pad(t, ((0, pad), (0, 0))).reshape(n_chunks, MOE_CHUNK, D)
    c_p = jnp.pad(combine, ((0, pad), (0, 0))).reshape(n_chunks, MOE_CHUNK, N_EXPERTS)

    def chunk(args):
        tc, cc = args
        gu = jnp.einsum('td,edf->tef', tc, w_gu)
        a = jax.nn.silu(gu[..., :D_EXPERT]) * gu[..., D_EXPERT:] * cc[:, :, None].astype(tc.dtype)
        return jnp.einsum('tef,efd->td', a, w_down)

    out = lax.map(chunk, (t_p, c_p)).reshape(-1, D)[:T]
    return out.reshape(B, S, D)


def setup_inputs(seed: int = 0) -> dict:
    key = jax.random.key(seed)
    ks = jax.random.split(key, 24)

    def nrm(k, shape, scale):
        return jax.random.normal(k, shape, dtype=jnp.float32) * scale

    def gain(k, shape):
        return 1.0 + 0.02 * jax.random.normal(k, shape, dtype=jnp.float32)

    L = DEPTH
    return {
        "x": nrm(ks[0], (BATCH, SEQ, D_MODEL), 1.0),
        "rel_bias": nrm(ks[1], (REL_BUCKETS, DIFF_HEADS), 0.5),
        "ln1_g": gain(ks[2], (L, D_MODEL)),
        "w_in": nrm(ks[3], (L, D_MODEL, N_IN), D_MODEL ** -0.5),
        "diff_q_norm": gain(ks[4], (L, DIFF_QK_DIM)),
        "diff_k_norm": gain(ks[5], (L, DIFF_QK_DIM)),
        "diff_lambda": nrm(ks[6], (L, 4, DIFF_QK_DIM), 0.1),
        "diff_subln_g": gain(ks[7], (L, DIFF_V_DIM)),
        "mla_q_a_norm": gain(ks[8], (L, MLA_Q_RANK)),
        "mla_w_uq": nrm(ks[9], (L, MLA_Q_RANK, MLA_HEADS * (MLA_NOPE_DIM + MLA_ROPE_DIM)), MLA_Q_RANK ** -0.5),
        "mla_kv_a_norm": gain(ks[10], (L, MLA_KV_RANK)),
        "mla_w_ukv": nrm(ks[11], (L, MLA_KV_RANK, MLA_HEADS * (MLA_NOPE_DIM + MLA_V_DIM)), MLA_KV_RANK ** -0.5),
        "mla_q_norm": gain(ks[12], (L, MLA_NOPE_DIM + MLA_ROPE_DIM)),
        "mla_k_norm": gain(ks[13], (L, MLA_NOPE_DIM + MLA_ROPE_DIM)),
        "w_branch": nrm(ks[14], (L, N_BRANCH, BRANCH_WIDTH, D_MODEL), BRANCH_WIDTH ** -0.5),
        "b_gate": nrm(ks[15], (L, N_BRANCH * D_MODEL), 0.02),
        "w_out": nrm(ks[16], (L, D_MODEL, D_MODEL), D_MODEL ** -0.5),
        "ln2_g": gain(ks[17], (L, D_MODEL)),
        "router_group_w": nrm(ks[18], (L, D_MODEL, N_GROUPS), D_MODEL ** -0.5),
        "router_group_b": nrm(ks[19], (L, N_GROUPS), 0.01),
        "router_expert_w": nrm(ks[20], (L, D_MODEL, N_EXPERTS), D_MODEL ** -0.5),
        "router_expert_b": nrm(ks[21], (L, N_EXPERTS), 0.01),
        "expert_w_gu": nrm(ks[22], (L, N_EXPERTS, D_MODEL, 2 * D_EXPERT), D_MODEL ** -0.5),
        "expert_w_down": nrm(ks[23], (L, N_EXPERTS, D_EXPERT, D_MODEL), D_EXPERT ** -0.5),
    }


def reference(x, rel_bias, ln1_g, w_in, diff_q_norm, diff_k_norm, diff_lambda, diff_subln_g,
              mla_q_a_norm, mla_w_uq, mla_kv_a_norm, mla_w_ukv, mla_q_norm, mla_k_norm,
              w_branch, b_gate, w_out, ln2_g, router_group_w, router_group_b,
              router_expert_w, router_expert_b, expert_w_gu, expert_w_down):
    S = x.shape[1]
    pos = jnp.arange(S, dtype=jnp.float32)
    inv_freq = 1.0 / (ROPE_THETA ** (jnp.arange(0, MLA_ROPE_DIM, 2, dtype=jnp.float32) / MLA_ROPE_DIM))
    ang = pos[:, None] * inv_freq[None, :]
    cos = jnp.cos(ang).astype(x.dtype)
    sin = jnp.sin(ang).astype(x.dtype)
    for l in range(DEPTH):
        h = _rms(x, ln1_g[l])
        x = x + _mixer(h, l, rel_bias, cos, sin, w_in[l], diff_q_norm[l], diff_k_norm[l],
                       diff_lambda[l], diff_subln_g[l], mla_q_a_norm[l], mla_w_uq[l],
                       mla_kv_a_norm[l], mla_w_ukv[l], mla_q_norm[l], mla_k_norm[l],
                       w_branch[l], b_gate[l], w_out[l])
        h = _rms(x, ln2_g[l])
        x = x + _hier_moe(h, router_group_w[l], router_group_b[l], router_expert_w[l],
                          router_expert_b[l], expert_w_gu[l], expert_w_down[l])
    return x
```

```python
import functools
import math

import jax
import jax.numpy as jnp
from jax import lax
from jax.experimental import pallas as pl
from jax.experimental.pallas import tpu as pltpu

F32 = jnp.float32
BF16 = jnp.bfloat16
I32 = jnp.int32

EPS = 1e-6
LANES = 128
HEAD_W = 128
DIFF_HEADS = 8
DIFF_QK_DIM = 64
SB_HEADS = 8
SB_HEAD_DIM = 128
MLA_HEADS = 8
MLA_Q_RANK = 512
MLA_KV_RANK = 256
MLA_NOPE_DIM = 128
MLA_ROPE_DIM = 64
MLA_V_DIM = 128
MLA_QK_PAD = 256
ROPE_THETA = 10000.0
REL_BUCKETS = 32
REL_MAX_DIST = 128
N_GROUPS = 8
EXPERTS_PER_GROUP = 4
N_EXPERTS = N_GROUPS * EXPERTS_PER_GROUP
N_BRANCH = 3
BRANCH_WIDTH = 1024
QKV_W = 6 * BRANCH_WIDTH
MLA_IN_W = 1024
NEG_BIG = -1e30
SB_SKIP_LOG = -100.0
VMEM_LIMIT = 56 * 1024 * 1024

_NT = (((1,), (1,)), ((), ()))


def _cparams(sem, vmem=VMEM_LIMIT):
    return pltpu.CompilerParams(dimension_semantics=sem, vmem_limit_bytes=vmem)


def _tile(n, target):
    t = min(n, target)
    while n % t:
        t -= 1
    return t


def _round_up(n, m):
    return -(-n // m) * m


def _rms_mm_body(x_ref, g_ref, w_ref, o_ref, h_ref):
    @pl.when(pl.program_id(1) == 0)
    def _():
        x = x_ref[...]
        ms = jnp.mean(x * x, axis=-1, keepdims=True)
        h_ref[...] = (x * lax.rsqrt(ms + EPS) * g_ref[...]).astype(BF16)

    o_ref[...] = jnp.dot(h_ref[...], w_ref[...], preferred_element_type=F32).astype(o_ref.dtype)


def _rms_matmul(x, g, w, out_dtype, name):
    T, K = x.shape
    N = w.shape[1]
    tm, tn = _tile(T, 1024), _tile(N, 1024)
    return pl.pallas_call(
        _rms_mm_body,
        out_shape=jax.ShapeDtypeStruct((T, N), out_dtype),
        grid=(T // tm, N // tn),
        in_specs=[pl.BlockSpec((tm, K), lambda i, j: (i, 0)),
                  pl.BlockSpec((1, K), lambda i, j: (0, 0)),
                  pl.BlockSpec((K, tn), lambda i, j: (0, j))],
        out_specs=pl.BlockSpec((tm, tn), lambda i, j: (i, j)),
        scratch_shapes=[pltpu.VMEM((tm, K), BF16)],
        compiler_params=_cparams(("parallel", "arbitrary")),
        name=name,
    )(x, g.reshape(1, K), w)


def _softmax_step(s, m_ref, l_ref, acc_ref, v):
    m_prev = m_ref[...]
    m_new = jnp.maximum(m_prev, jnp.max(s, axis=-1, keepdims=True))
    alpha = jnp.exp(m_prev - m_new)
    p = jnp.exp(s - m_new)
    l_ref[...] = alpha * l_ref[...] + jnp.sum(p, axis=-1, keepdims=True)
    acc_ref[...] = alpha * acc_ref[...] + jnp.dot(p.astype(BF16), v, preferred_element_type=F32)
    m_ref[...] = m_new


def _softmax_init(m_ref, l_ref, acc_ref):
    m_ref[...] = jnp.full(m_ref.shape, NEG_BIG, F32)
    l_ref[...] = jnp.zeros(l_ref.shape, F32)
    acc_ref[...] = jnp.zeros(acc_ref.shape, F32)


def _t5_tiles_body(rb_ref, o_ref, *, tq):
    h = pl.program_id(0)
    row = lax.broadcasted_iota(I32, (tq, tq), 0)
    col = lax.broadcasted_iota(I32, (tq, tq), 1)
    max_exact = REL_BUCKETS // 2
    far = rb_ref[(REL_BUCKETS - 1) * DIFF_HEADS + h]
    for which in range(2):
        d = row - col + which * tq
        n = jnp.maximum(d, 0)
        nf = jnp.maximum(n, 1).astype(F32)
        large = max_exact + (jnp.log(nf / max_exact) / math.log(REL_MAX_DIST / max_exact)
                             * (REL_BUCKETS - max_exact)).astype(I32)
        large = jnp.minimum(large, REL_BUCKETS - 1)
        bucket = jnp.where(n < max_exact, n, large)
        val = jnp.zeros((tq, tq), F32)
        for b in range(REL_BUCKETS - 1):
            val = jnp.where(bucket == b, rb_ref[b * DIFF_HEADS + h] - far, val)
        o_ref[0, which] = jnp.where(d >= 0, val, NEG_BIG)


def _t5_bias_tiles(rel_bias, tq):
    assert tq >= REL_MAX_DIST
    return pl.pallas_call(
        functools.partial(_t5_tiles_body, tq=tq),
        out_shape=jax.ShapeDtypeStruct((DIFF_HEADS, 2, tq, tq), F32),
        grid=(DIFF_HEADS,),
        in_specs=[pl.BlockSpec(memory_space=pltpu.SMEM)],
        out_specs=pl.BlockSpec((1, 2, tq, tq), lambda h: (h, 0, 0, 0)),
        compiler_params=_cparams(("parallel",)),
        name="t5_bias_tiles",
    )(rel_bias.reshape(-1))


def _diff_attn_body(lam_ref, qg_ref, kg_ref, sg_ref, bias_ref, q_ref, k_ref, v_ref, o_ref,
                    kn_s, m1, l1, a1, m2, l2, a2, *, tq, seq):
    qi = pl.program_id(2)
    lo = lax.broadcasted_iota(I32, (1, HEAD_W), 1) < DIFF_QK_DIM

    def half_rms(x, g):
        x2 = x * x
        s_lo = jnp.sum(jnp.where(lo, x2, 0.0), axis=-1, keepdims=True)
        s_hi = jnp.sum(jnp.where(lo, 0.0, x2), axis=-1, keepdims=True)
        r = lax.rsqrt(jnp.where(lo, s_lo, s_hi) * (1.0 / DIFF_QK_DIM) + EPS)
        return x * r * g

    @pl.when(qi == 0)
    def _():
        def norm_chunk(c, carry):
            rows = pl.ds(pl.multiple_of(c * tq, tq), tq)
            kn_s[rows, :] = half_rms(k_ref[rows, :].astype(F32), kg_ref[...]).astype(BF16)
            return carry
        lax.fori_loop(0, seq // tq, norm_chunk, 0)

    qn = half_rms(q_ref[...].astype(F32), qg_ref[...]) * (DIFF_QK_DIM ** -0.5)
    q1 = jnp.where(lo, qn, 0.0).astype(BF16)
    q2 = jnp.where(lo, 0.0, qn).astype(BF16)

    _softmax_init(m1, l1, a1)
    _softmax_init(m2, l2, a2)

    def step(j, bias):
        rows = pl.ds(pl.multiple_of(j * tq, tq), tq)
        kb = kn_s[rows, :]
        vb = v_ref[rows, :]
        s1 = lax.dot_general(q1, kb, _NT, preferred_element_type=F32)
        s2 = lax.dot_general(q2, kb, _NT, preferred_element_type=F32)
        if bias is not None:
            s1 = s1 + bias
            s2 = s2 + bias
        _softmax_step(s1, m1, l1, a1, vb)
        _softmax_step(s2, m2, l2, a2, vb)

    def far_step(j, carry):
        step(j, None)
        return carry
    lax.fori_loop(0, jnp.maximum(qi - 1, 0), far_step, 0)

    @pl.when(qi >= 1)
    def _():
        step(qi - 1, bias_ref[0, 1])

    step(qi, bias_ref[0, 0])

    lp = lam_ref[...]
    lam_init = lp[4:5, 0:1]
    lam = (jnp.exp(jnp.sum(lp[0:1] * lp[1:2], axis=-1, keepdims=True))
           - jnp.exp(jnp.sum(lp[2:3] * lp[3:4], axis=-1, keepdims=True)) + lam_init)
    o = a1[...] / l1[...] - lam * (a2[...] / l2[...])
    ms = jnp.mean(o * o, axis=-1, keepdims=True)
    o = o * lax.rsqrt(ms + EPS) * sg_ref[...] * (1.0 - lam_init)
    o_ref[...] = o.astype(BF16)


def _diff_attention(proj, lam_tab, q_gain, k_gain, sub_gain, bias_tiles, batch, seq, tq):
    T = proj.shape[0]
    nq = seq // tq
    H = DIFF_HEADS
    small = lambda b, h, i: (0, 0)
    return pl.pallas_call(
        functools.partial(_diff_attn_body, tq=tq, seq=seq),
        out_shape=jax.ShapeDtypeStruct((T, BRANCH_WIDTH), BF16),
        grid=(batch, H, nq),
        in_specs=[pl.BlockSpec((8, DIFF_QK_DIM), small),
                  pl.BlockSpec((1, HEAD_W), small),
                  pl.BlockSpec((1, HEAD_W), small),
                  pl.BlockSpec((1, HEAD_W), small),
                  pl.BlockSpec((1, 2, tq, tq), lambda b, h, i: (h, 0, 0, 0)),
                  pl.BlockSpec((tq, HEAD_W), lambda b, h, i: (b * nq + i, h)),
                  pl.BlockSpec((seq, HEAD_W), lambda b, h, i: (b, H + h)),
                  pl.BlockSpec((seq, HEAD_W), lambda b, h, i: (b, 2 * H + h))],
        out_specs=pl.BlockSpec((tq, HEAD_W), lambda b, h, i: (b * nq + i, h)),
        scratch_shapes=[pltpu.VMEM((seq, HEAD_W), BF16),
                        pltpu.VMEM((tq, 1), F32), pltpu.VMEM((tq, 1), F32), pltpu.VMEM((tq, HEAD_W), F32),
                        pltpu.VMEM((tq, 1), F32), pltpu.VMEM((tq, 1), F32), pltpu.VMEM((tq, HEAD_W), F32)],
        compiler_params=_cparams(("parallel", "parallel", "arbitrary")),
        name="diff_attn",
    )(lam_tab, q_gain, k_gain, sub_gain, bias_tiles, proj, proj, proj)


def _sb_attn_body(q_ref, k_ref, v_ref, o_ref, c_s, acc_s, *, tq):
    qi = pl.program_id(2)
    q = q_ref[...]
    row = lax.broadcasted_iota(I32, (tq, tq), 0)
    col = lax.broadcasted_iota(I32, (tq, tq), 1)
    tri = jnp.where(row >= col, 1.0, 0.0).astype(BF16)
    tri2 = jnp.concatenate([tri, tri], axis=0)
    strict = col < row
    scale = SB_HEAD_DIM ** -0.5

    c_s[...] = jnp.zeros(c_s.shape, F32)
    acc_s[...] = jnp.zeros(acc_s.shape, F32)

    def block(j, masked):
        rows = pl.ds(pl.multiple_of(j * tq, tq), tq)
        z = lax.dot_general(q, k_ref[rows, :], _NT, preferred_element_type=F32) * scale
        log_fail = -(jnp.maximum(z, 0.0) + jnp.log(1.0 + jnp.exp(-jnp.abs(z))))
        if masked:
            log_fail = jnp.where(strict, log_fail, 0.0)
        hi = log_fail.astype(BF16)
        lo = (log_fail - hi.astype(F32)).astype(BF16)
        cum = jnp.dot(jnp.concatenate([hi, lo], axis=1), tri2, preferred_element_type=F32)
        c = c_s[...]
        a = jnp.exp(z + cum + c)
        if masked:
            a = jnp.where(strict, a, 0.0)
        acc_s[...] += jnp.dot(a.astype(BF16), v_ref[rows, :], preferred_element_type=F32)
        c_new = c + cum[:, 0:1]
        c_s[...] = c_new
        return jnp.max(c_new)

    c_max = block(qi, True)

    def cond(carry):
        j, c_max = carry
        return jnp.logical_and(j >= 0, c_max > SB_SKIP_LOG)

    def body(carry):
        j, _ = carry
        return j - 1, block(j, False)

    lax.while_loop(cond, body, (qi - 1, c_max))
    o_ref[...] = acc_s[...].astype(BF16)


def _sb_attention(proj, batch, seq, tq):
    T = proj.shape[0]
    nq = seq // tq
    H = SB_HEADS
    base = 3 * DIFF_HEADS
    return pl.pallas_call(
        functools.partial(_sb_attn_body, tq=tq),
        out_shape=jax.ShapeDtypeStruct((T, BRANCH_WIDTH), BF16),
        grid=(batch, H, nq),
        in_specs=[pl.BlockSpec((tq, HEAD_W), lambda b, h, i: (b * nq + i, base + h)),
                  pl.BlockSpec((seq, HEAD_W), lambda b, h, i: (b, base + H + h)),
                  pl.BlockSpec((seq, HEAD_W), lambda b, h, i: (b, base + 2 * H + h))],
        out_specs=pl.BlockSpec((tq, HEAD_W), lambda b, h, i: (b * nq + i, h)),
        scratch_shapes=[pltpu.VMEM((tq, 1), F32), pltpu.VMEM((tq, HEAD_W), F32)],
        compiler_params=_cparams(("parallel", "parallel", "arbitrary")),
        name="sb_attn",
    )(proj, proj, proj)


def _mla_prep_body(pm_ref, qa_ref, kva_ref, wuq_ref, wukv_ref, qg_ref, kg_ref,
                   cos_ref, sina_ref, sinb_ref, q_ref, k_ref, v_ref):
    x = pm_ref[...].astype(F32)

    def rms(t, g):
        return t * lax.rsqrt(jnp.mean(t * t, axis=-1, keepdims=True) + EPS) * g

    cq = rms(x[:, :MLA_Q_RANK], qa_ref[...]).astype(BF16)
    ckv = rms(x[:, MLA_Q_RANK:MLA_Q_RANK + MLA_KV_RANK], kva_ref[...]).astype(BF16)
    kr = x[:, MLA_Q_RANK + MLA_KV_RANK:MLA_Q_RANK + MLA_KV_RANK + LANES]
    q = jnp.dot(cq, wuq_ref[...], preferred_element_type=F32)
    kv = jnp.dot(ckv, wukv_ref[...], preferred_element_type=F32)
    cos, sina, sinb = cos_ref[...], sina_ref[...], sinb_ref[...]
    half = MLA_ROPE_DIM // 2

    def rope(pe):
        return pe * cos + pltpu.roll(pe, LANES - half, 1) * sina + pltpu.roll(pe, half, 1) * sinb

    k_pe = rope(kr)
    k_pe_ss = jnp.sum(k_pe * k_pe, axis=-1, keepdims=True)
    qk_dim = MLA_NOPE_DIM + MLA_ROPE_DIM
    qg, kg = qg_ref[...], kg_ref[...]
    for h in range(MLA_HEADS):
        o = h * MLA_QK_PAD
        q_n = q[:, o:o + MLA_NOPE_DIM]
        q_pe = rope(q[:, o + MLA_NOPE_DIM:o + MLA_QK_PAD])
        ss = jnp.sum(q_n * q_n, axis=-1, keepdims=True) + jnp.sum(q_pe * q_pe, axis=-1, keepdims=True)
        r = lax.rsqrt(ss * (1.0 / qk_dim) + EPS) * (qk_dim ** -0.5)
        q_ref[:, o:o + MLA_NOPE_DIM] = (q_n * r * qg[:, :MLA_NOPE_DIM]).astype(BF16)
        q_ref[:, o + MLA_NOPE_DIM:o + MLA_QK_PAD] = (q_pe * r * qg[:, MLA_NOPE_DIM:]).astype(BF16)
        k_n = kv[:, o:o + MLA_NOPE_DIM]
        ssk = jnp.sum(k_n * k_n, axis=-1, keepdims=True) + k_pe_ss
        rk = lax.rsqrt(ssk * (1.0 / qk_dim) + EPS)
        k_ref[:, o:o + MLA_NOPE_DIM] = (k_n * rk * kg[:, :MLA_NOPE_DIM]).astype(BF16)
        k_ref[:, o + MLA_NOPE_DIM:o + MLA_QK_PAD] = (k_pe * rk * kg[:, MLA_NOPE_DIM:]).astype(BF16)
        v_ref[:, h * MLA_V_DIM:(h + 1) * MLA_V_DIM] = kv[:, o + MLA_NOPE_DIM:o + MLA_QK_PAD].astype(BF16)


def _mla_prep(proj, mla_blk, qa_g, kva_g, w_uq, w_ukv, q_g, k_g, cos_t, sina_t, sinb_t, seq):
    T = proj.shape[0]
    tm = _tile(seq, 512)
    ns = seq // tm
    W = MLA_HEADS * MLA_QK_PAD
    const = lambda i: (0, 0)
    pos = lambda i: (i % ns, 0)
    return pl.pallas_call(
        _mla_prep_body,
        out_shape=(jax.ShapeDtypeStruct((T, W), BF16), jax.ShapeDtypeStruct((T, W), BF16),
                   jax.ShapeDtypeStruct((T, MLA_HEADS * MLA_V_DIM), BF16)),
        grid=(T // tm,),
        in_specs=[pl.BlockSpec((tm, MLA_IN_W), lambda i: (i, mla_blk)),
                  pl.BlockSpec((1, MLA_Q_RANK), const),
                  pl.BlockSpec((1, MLA_KV_RANK), const),
                  pl.BlockSpec((MLA_Q_RANK, W), const),
                  pl.BlockSpec((MLA_KV_RANK, W), const),
                  pl.BlockSpec((1, MLA_QK_PAD), const),
                  pl.BlockSpec((1, MLA_QK_PAD), const),
                  pl.BlockSpec((tm, LANES), pos),
                  pl.BlockSpec((tm, LANES), pos),
                  pl.BlockSpec((tm, LANES), pos)],
        out_specs=(pl.BlockSpec((tm, W), lambda i: (i, 0)),
                   pl.BlockSpec((tm, W), lambda i: (i, 0)),
                   pl.BlockSpec((tm, MLA_HEADS * MLA_V_DIM), lambda i: (i, 0))),
        compiler_params=_cparams(("parallel",)),
        name="mla_prep",
    )(proj, qa_g, kva_g, w_uq, w_ukv, q_g, k_g, cos_t, sina_t, sinb_t)


def _mla_attn_body(q_ref, k_ref, v_ref, o_ref, m, l, acc, *, tq):
    qi = pl.program_id(2)
    q = q_ref[...]
    _softmax_init(m, l, acc)

    def step(j, masked):
        rows = pl.ds(pl.multiple_of(j * tq, tq), tq)
        s = lax.dot_general(q, k_ref[rows, :], _NT, preferred_element_type=F32)
        if masked:
            row = lax.broadcasted_iota(I32, (tq, tq), 0)
            col = lax.broadcasted_iota(I32, (tq, tq), 1)
            s = jnp.where(col <= row, s, NEG_BIG)
        _softmax_step(s, m, l, acc, v_ref[rows, :])

    def far_step(j, carry):
        step(j, False)
        return carry
    lax.fori_loop(0, qi, far_step, 0)
    step(qi, True)
    o_ref[...] = (acc[...] / l[...]).astype(BF16)


def _mla_attention(q, k, v, batch, seq, tq):
    T = q.shape[0]
    nq = seq // tq
    return pl.pallas_call(
        functools.partial(_mla_attn_body, tq=tq),
        out_shape=jax.ShapeDtypeStruct((T, BRANCH_WIDTH), BF16),
        grid=(batch, MLA_HEADS, nq),
        in_specs=[pl.BlockSpec((tq, MLA_QK_PAD), lambda b, h, i: (b * nq + i, h)),
                  pl.BlockSpec((seq, MLA_QK_PAD), lambda b, h, i: (b, h)),
                  pl.BlockSpec((seq, MLA_V_DIM), lambda b, h, i: (b, h))],
        out_specs=pl.BlockSpec((tq, MLA_V_DIM), lambda b, h, i: (b * nq + i, h)),
        scratch_shapes=[pltpu.VMEM((tq, 1), F32), pltpu.VMEM((tq, 1), F32), pltpu.VMEM((tq, MLA_V_DIM), F32)],
        compiler_params=_cparams(("parallel", "parallel", "arbitrary")),
        name="mla_attn",
    )(q, k, v)


def _merge_body(oa_ref, ob_ref, oc_ref, wb_ref, ga_ref, gb_ref, gc_ref, bg_ref, o_ref):
    acc = None
    for b, (o_r, g_r) in enumerate(((oa_ref, ga_ref), (ob_ref, gb_ref), (oc_ref, gc_ref))):
        gate = jax.nn.sigmoid(g_r[...].astype(F32) + bg_ref[b:b + 1, :])
        t = gate * jnp.dot(o_r[...], wb_ref[b], preferred_element_type=F32)
        acc = t if acc is None else acc + t
    o_ref[...] = acc.astype(BF16)


def _branch_merge(o_a, o_b, o_c, w_branch, proj, b_gate, d_model):
    T = o_a.shape[0]
    tm, tn = _tile(T, 1024), _tile(d_model, 512)
    assert QKV_W % tn == 0
    g0 = QKV_W // tn
    nj = d_model // tn
    o_spec = pl.BlockSpec((tm, BRANCH_WIDTH), lambda i, j: (i, 0))
    gate_spec = lambda b: pl.BlockSpec((tm, tn), lambda i, j: (i, g0 + b * nj + j))
    return pl.pallas_call(
        _merge_body,
        out_shape=jax.ShapeDtypeStruct((T, d_model), BF16),
        grid=(T // tm, nj),
        in_specs=[o_spec, o_spec, o_spec,
                  pl.BlockSpec((N_BRANCH, BRANCH_WIDTH, tn), lambda i, j: (0, 0, j)),
                  gate_spec(0), gate_spec(1), gate_spec(2),
                  pl.BlockSpec((N_BRANCH, tn), lambda i, j: (0, j))],
        out_specs=pl.BlockSpec((tm, tn), lambda i, j: (i, j)),
        compiler_params=_cparams(("parallel", "arbitrary")),
        name="branch_merge",
    )(o_a, o_b, o_c, w_branch, proj, proj, proj, b_gate)


def _out_proj_body(a_ref, w_ref, r_ref, o_ref):
    o_ref[...] = r_ref[...] + jnp.dot(a_ref[...], w_ref[...], preferred_element_type=F32)


def _out_proj(merged, w_out, x):
    T, K = merged.shape
    N = w_out.shape[1]
    tm, tn = _tile(T, 1024), _tile(N, 1024)
    return pl.pallas_call(
        _out_proj_body,
        out_shape=jax.ShapeDtypeStruct((T, N), F32),
        grid=(T // tm, N // tn),
        in_specs=[pl.BlockSpec((tm, K), lambda i, j: (i, 0)),
                  pl.BlockSpec((K, tn), lambda i, j: (0, j)),
                  pl.BlockSpec((tm, tn), lambda i, j: (i, j))],
        out_specs=pl.BlockSpec((tm, tn), lambda i, j: (i, j)),
        compiler_params=_cparams(("parallel", "arbitrary")),
        name="out_proj",
    )(merged, w_out, x)


def _route_body(x_ref, g_ref, wr_ref, br_ref, h_ref, comb_ref, meta_ref, cnt_ref, carry_s, *, tm):
    i = pl.program_id(0)

    @pl.when(i == 0)
    def _():
        carry_s[...] = jnp.zeros(carry_s.shape, F32)

    x = x_ref[...]
    h = x * lax.rsqrt(jnp.mean(x * x, axis=-1, keepdims=True) + EPS) * g_ref[...]
    h_ref[...] = h
    logits = jnp.dot(h, wr_ref[...], preferred_element_type=F32,
                     precision=lax.Precision.HIGHEST) + br_ref[...]
    lane = lax.broadcasted_iota(I32, (tm, LANES), 1)
    big = jnp.int32(LANES)

    gl = jnp.where(lane < N_GROUPS, logits, -jnp.inf)
    g_max = jnp.max(gl, axis=-1, keepdims=True)
    g_idx = jnp.min(jnp.where(gl == g_max, lane, big), axis=-1, keepdims=True)
    g_val = 1.0 / jnp.sum(jnp.exp(gl - g_max), axis=-1, keepdims=True)

    e_lo = N_GROUPS + g_idx * EXPERTS_PER_GROUP
    el = jnp.where(jnp.logical_and(lane >= e_lo, lane < e_lo + EXPERTS_PER_GROUP), logits, -jnp.inf)
    e1 = jnp.max(el, axis=-1, keepdims=True)
    i1 = jnp.min(jnp.where(el == e1, lane, big), axis=-1, keepdims=True)
    el2 = jnp.where(lane == i1, -jnp.inf, el)
    e2 = jnp.max(el2, axis=-1, keepdims=True)
    i2 = jnp.min(jnp.where(el2 == e2, lane, big), axis=-1, keepdims=True)
    t = jnp.exp(e2 - e1)
    w1 = g_val / (1.0 + t)
    w2 = g_val * t / (1.0 + t)
    comb_ref[...] = jnp.where(lane == i1, w1, 0.0) + jnp.where(lane == i2, w2, 0.0)

    onehot = jnp.where(lane == g_idx, 1.0, 0.0)
    r = lax.broadcasted_iota(I32, (tm, tm), 0)
    c = lax.broadcasted_iota(I32, (tm, tm), 1)
    before = jnp.where(c < r, 1.0, 0.0).astype(BF16)
    prefix = jnp.dot(before, onehot.astype(BF16), preferred_element_type=F32) + carry_s[...]
    rank = jnp.sum(jnp.where(lane == g_idx, prefix, 0.0), axis=-1, keepdims=True)
    carry_s[...] += jnp.sum(onehot, axis=0, keepdims=True)
    meta_ref[...] = jnp.where(lane == 0, g_idx, jnp.where(lane == 1, rank.astype(I32), 0))
    cnt_ref[...] = jnp.broadcast_to(carry_s[...], cnt_ref.shape).astype(I32)


def _moe_route(x, g, w_r, b_r):
    T, D = x.shape
    tm = _tile(T, 512)
    return pl.pallas_call(
        functools.partial(_route_body, tm=tm),
        out_shape=(jax.ShapeDtypeStruct((T, D), F32), jax.ShapeDtypeStruct((T, LANES), F32),
                   jax.ShapeDtypeStruct((T, LANES), I32), jax.ShapeDtypeStruct((8, LANES), I32)),
        grid=(T // tm,),
        in_specs=[pl.BlockSpec((tm, D), lambda i: (i, 0)),
                  pl.BlockSpec((1, D), lambda i: (0, 0)),
                  pl.BlockSpec((D, LANES), lambda i: (0, 0)),
                  pl.BlockSpec((1, LANES), lambda i: (0, 0))],
        out_specs=(pl.BlockSpec((tm, D), lambda i: (i, 0)),
                   pl.BlockSpec((tm, LANES), lambda i: (i, 0)),
                   pl.BlockSpec((tm, LANES), lambda i: (i, 0)),
                   pl.BlockSpec((8, LANES), lambda i: (0, 0))),
        scratch_shapes=[pltpu.VMEM((1, LANES), F32)],
        compiler_params=_cparams(("arbitrary",)),
        name="moe_route",
    )(x, g.reshape(1, D), w_r, b_r)


def _dispatch_body(dest_ref, h_ref, c_ref, xs_in, cs_in, xs_ref, cs_ref, sem, *, chunk):
    del xs_in, cs_in
    base = pl.program_id(0) * chunk

    def row_copies(t, d):
        return (pltpu.make_async_copy(h_ref.at[pl.ds(t, 1)], xs_ref.at[pl.ds(d, 1)], sem.at[0]),
                pltpu.make_async_copy(c_ref.at[pl.ds(t, 1)], cs_ref.at[pl.ds(d, 1)], sem.at[1]))

    def issue(r, carry):
        for cp in row_copies(base + r, dest_ref[base + r]):
            cp.start()
        return carry
    lax.fori_loop(0, chunk, issue, 0)

    def drain(r, carry):
        for cp in row_copies(base + r, dest_ref[base + r]):
            cp.wait()
        return carry
    lax.fori_loop(0, chunk, drain, 0)


def _moe_dispatch(dest, h, comb, t_pad):
    T, D = h.shape
    chunk = _tile(T, 1024)
    any_spec = pl.BlockSpec(memory_space=pl.ANY)
    return pl.pallas_call(
        functools.partial(_dispatch_body, chunk=chunk),
        out_shape=(jax.ShapeDtypeStruct((t_pad, D), F32), jax.ShapeDtypeStruct((t_pad, LANES), F32)),
        grid_spec=pltpu.PrefetchScalarGridSpec(
            num_scalar_prefetch=1, grid=(T // chunk,),
            in_specs=[any_spec, any_spec, any_spec, any_spec],
            out_specs=(any_spec, any_spec),
            scratch_shapes=[pltpu.SemaphoreType.DMA((2,))]),
        input_output_aliases={3: 0, 4: 1},
        compiler_params=_cparams(("arbitrary",)),
        name="moe_dispatch",
    )(dest, h, comb, jnp.zeros((t_pad, D), F32), jnp.zeros((t_pad, LANES), F32))


def _experts_body(tg_ref, nt_ref, xs_ref, cs_ref, wgu_ref, wd_ref, y_ref, *, d_expert):
    i, e = pl.program_id(0), pl.program_id(1)
    valid = i < nt_ref[0]

    @pl.when(valid)
    def _():
        x = xs_ref[...].astype(BF16)
        gu = jnp.dot(x, wgu_ref[0], preferred_element_type=F32)
        lane = lax.broadcasted_iota(I32, cs_ref.shape, 1)
        sel = N_GROUPS + tg_ref[i] * EXPERTS_PER_GROUP + e
        c = jnp.sum(jnp.where(lane == sel, cs_ref[...], 0.0), axis=-1, keepdims=True)
        a = (jax.nn.silu(gu[:, :d_expert]) * gu[:, d_expert:] * c).astype(BF16)
        y = jnp.dot(a, wd_ref[0], preferred_element_type=F32)

        @pl.when(e == 0)
        def _():
            y_ref[...] = y

        @pl.when(e > 0)
        def _():
            y_ref[...] += y

    @pl.when(jnp.logical_and(jnp.logical_not(valid), e == 0))
    def _():
        y_ref[...] = jnp.zeros(y_ref.shape, F32)


def _moe_experts(tile_group, n_tiles, xs, cs, w_gu, w_down, tg):
    t_pad, D = xs.shape
    d_expert = w_down.shape[1]
    expert = lambda i, e, tgr, ntr: (tgr[i] * EXPERTS_PER_GROUP + e, 0, 0)
    rows = lambda i, e, tgr, ntr: (i, 0)
    return pl.pallas_call(
        functools.partial(_experts_body, d_expert=d_expert),
        out_shape=jax.ShapeDtypeStruct((t_pad, D), F32),
        grid_spec=pltpu.PrefetchScalarGridSpec(
            num_scalar_prefetch=2, grid=(t_pad // tg, EXPERTS_PER_GROUP),
            in_specs=[pl.BlockSpec((tg, D), rows),
                      pl.BlockSpec((tg, LANES), rows),
                      pl.BlockSpec((1, D, 2 * d_expert), expert),
                      pl.BlockSpec((1, d_expert, D), expert)],
            out_specs=pl.BlockSpec((tg, D), rows)),
        compiler_params=_cparams(("parallel", "arbitrary")),
        name="moe_experts",
    )(tile_group, n_tiles, xs, cs, w_gu, w_down)


def _combine_body(dest_ref, x_ref, ys_ref, o_ref, buf, sem, *, tm):
    base = pl.program_id(0) * tm

    def row_copy(r):
        return pltpu.make_async_copy(ys_ref.at[pl.ds(dest_ref[base + r], 1)], buf.at[pl.ds(r, 1)], sem.at[0])

    def issue(r, carry):
        row_copy(r).start()
        return carry
    lax.fori_loop(0, tm, issue, 0)

    def drain(r, carry):
        row_copy(r).wait()
        return carry
    lax.fori_loop(0, tm, drain, 0)
    o_ref[...] = x_ref[...] + buf[...]


def _moe_combine(dest, x, ys):
    T, D = x.shape
    tm = _tile(T, 256)
    return pl.pallas_call(
        functools.partial(_combine_body, tm=tm),
        out_shape=jax.ShapeDtypeStruct((T, D), F32),
        grid_spec=pltpu.PrefetchScalarGridSpec(
            num_scalar_prefetch=1, grid=(T // tm,),
            in_specs=[pl.BlockSpec((tm, D), lambda i, d: (i, 0)),
                      pl.BlockSpec(memory_space=pl.ANY)],
            out_specs=pl.BlockSpec((tm, D), lambda i, d: (i, 0)),
            scratch_shapes=[pltpu.VMEM((tm, D), F32), pltpu.SemaphoreType.DMA((1,))]),
        compiler_params=_cparams(("arbitrary",)),
        name="moe_combine",
    )(dest, x, ys)


def _hier_moe(x, ln_g, w_r, b_r, w_gu, w_down):
    T, D = x.shape
    tg = _tile(T, 512)
    h, comb, meta, counts = _moe_route(x, ln_g, w_r, b_r)
    cnt = counts[0, :N_GROUPS]
    seg_tiles = (cnt + tg - 1) // tg
    tile_end = jnp.cumsum(seg_tiles)
    seg_off = (tile_end - seg_tiles) * tg
    dest = seg_off[meta[:, 0]] + meta[:, 1]
    n_tiles_max = T // tg + N_GROUPS
    n_tiles = tile_end[-1]
    tile_ids = jnp.arange(n_tiles_max, dtype=I32)
    tile_group = jnp.searchsorted(tile_end, jnp.minimum(tile_ids, n_tiles - 1), side="right").astype(I32)
    xs, cs = _moe_dispatch(dest.astype(I32), h, comb, n_tiles_max * tg)
    ys = _moe_experts(tile_group, n_tiles.reshape(1).astype(I32), xs, cs, w_gu, w_down, tg)
    return _moe_combine(dest.astype(I32), x, ys)


def _layout_w_in(w_in, d_model):
    mla_w = MLA_Q_RANK + MLA_KV_RANK + MLA_ROPE_DIM
    gate_w = N_BRANCH * d_model
    mla_off = _round_up(QKV_W + gate_w, MLA_IN_W)
    n_pad = _round_up(mla_off + MLA_IN_W, 1024)
    w = jnp.concatenate([
        w_in[:, :QKV_W],
        w_in[:, QKV_W + mla_w:],
        jnp.zeros((d_model, mla_off - QKV_W - gate_w), w_in.dtype),
        w_in[:, QKV_W:QKV_W + mla_w],
        jnp.zeros((d_model, n_pad - mla_off - mla_w), w_in.dtype)], axis=1)
    return w.astype(BF16), mla_off // MLA_IN_W


def _layout_w_uq(w_uq):
    w = w_uq.reshape(MLA_Q_RANK, MLA_HEADS, MLA_NOPE_DIM + MLA_ROPE_DIM)
    w = jnp.pad(w, ((0, 0), (0, 0), (0, MLA_QK_PAD - MLA_NOPE_DIM - MLA_ROPE_DIM)))
    return w.reshape(MLA_Q_RANK, MLA_HEADS * MLA_QK_PAD).astype(BF16)


def _pad_gain(g):
    return jnp.pad(g, (0, MLA_QK_PAD - g.shape[0])).reshape(1, MLA_QK_PAD)


def _rope_tables(seq):
    pos = jnp.arange(seq, dtype=F32)
    inv_freq = 1.0 / (ROPE_THETA ** (jnp.arange(0, MLA_ROPE_DIM, 2, dtype=F32) / MLA_ROPE_DIM))
    ang = pos[:, None] * inv_freq[None, :]
    cos, sin = jnp.cos(ang), jnp.sin(ang)
    z = jnp.zeros_like(cos)
    return (jnp.concatenate([cos, cos, z, z], axis=1),
            jnp.concatenate([-sin, z, z, z], axis=1),
            jnp.concatenate([z, sin, z, z], axis=1))


def kernel(x, rel_bias, ln1_g, w_in, diff_q_norm, diff_k_norm, diff_lambda, diff_subln_g, mla_q_a_norm, mla_w_uq, mla_kv_a_norm, mla_w_ukv, mla_q_norm, mla_k_norm, w_branch, b_gate, w_out, ln2_g, router_group_w, router_group_b, router_expert_w, router_expert_b, expert_w_gu, expert_w_down):
    B, S, D = x.shape
    depth = w_in.shape[0]
    T = B * S
    tq = _tile(S, 256)
    cos_t, sina_t, sinb_t = _rope_tables(S)
    bias_tiles = _t5_bias_tiles(rel_bias, tq)
    xt = x.reshape(T, D)
    for l in range(depth):
        w_in_l, mla_blk = _layout_w_in(w_in[l], D)
        proj = _rms_matmul(xt, ln1_g[l], w_in_l, BF16, "rms_in_proj")

        lam_init = 0.8 - 0.6 * math.exp(-0.3 * l)
        lam_tab = jnp.concatenate([diff_lambda[l], jnp.full((4, DIFF_QK_DIM), lam_init, F32)], axis=0)
        o_a = _diff_attention(proj, lam_tab,
                              jnp.tile(diff_q_norm[l], 2).reshape(1, HEAD_W),
                              jnp.tile(diff_k_norm[l], 2).reshape(1, HEAD_W),
                              diff_subln_g[l].reshape(1, HEAD_W), bias_tiles, B, S, tq)
        o_b = _sb_attention(proj, B, S, tq)
        q_c, k_c, v_c = _mla_prep(proj, mla_blk, mla_q_a_norm[l].reshape(1, -1), mla_kv_a_norm[l].reshape(1, -1),
                                  _layout_w_uq(mla_w_uq[l]), mla_w_ukv[l].astype(BF16),
                                  _pad_gain(mla_q_norm[l]), _pad_gain(mla_k_norm[l]),
                                  cos_t, sina_t, sinb_t, S)
        o_c = _mla_attention(q_c, k_c, v_c, B, S, tq)

        merged = _branch_merge(o_a, o_b, o_c, w_branch[l].astype(BF16), proj,
                               b_gate[l].reshape(N_BRANCH, D), D)
        xt = _out_proj(merged, w_out[l].astype(BF16), xt)

        w_r = jnp.concatenate([router_group_w[l], router_expert_w[l],
                               jnp.zeros((D, LANES - N_GROUPS - N_EXPERTS), F32)], axis=1)
        b_r = jnp.concatenate([router_group_b[l], router_expert_b[l],
                               jnp.zeros((LANES - N_GROUPS - N_EXPERTS,), F32)]).reshape(1, LANES)
        xt = _hier_moe(xt, ln2_g[l], w_r, b_r, expert_w_gu[l].astype(BF16), expert_w_down[l].astype(BF16))
    return xt.reshape(B, S, D)
```

```python
import functools
import math

import jax
import jax.numpy as jnp
from jax import lax
from jax.experimental import pallas as pl
from jax.experimental.pallas import tpu as pltpu

F32 = jnp.float32
BF16 = jnp.bfloat16
I32 = jnp.int32

EPS = 1e-6
LANES = 128
HEAD_W = 128
DIFF_HEADS = 8
DIFF_QK_DIM = 64
SB_HEADS = 8
SB_HEAD_DIM = 128
MLA_HEADS = 8
MLA_Q_RANK = 512
MLA_KV_RANK = 256
MLA_NOPE_DIM = 128
MLA_ROPE_DIM = 64
MLA_V_DIM = 128
MLA_QK_PAD = 256
ROPE_THETA = 10000.0
REL_BUCKETS = 32
REL_MAX_DIST = 128
N_GROUPS = 8
EXPERTS_PER_GROUP = 4
N_EXPERTS = N_GROUPS * EXPERTS_PER_GROUP
N_BRANCH = 3
BRANCH_WIDTH = 1024
QKV_W = 6 * BRANCH_WIDTH
MLA_IN_W = 1024
NEG_BIG = -1e30
LOG2_E = math.log2(math.e)
SB_SKIP_LOG = -100.0 * LOG2_E
VMEM_LIMIT = 56 * 1024 * 1024

_NT = (((1,), (1,)), ((), ()))


def _cparams(sem, vmem=VMEM_LIMIT):
    return pltpu.CompilerParams(dimension_semantics=sem, vmem_limit_bytes=vmem)


def _tile(n, target):
    t = min(n, target)
    while n % t:
        t -= 1
    return t


def _round_up(n, m):
    return -(-n // m) * m


def _rms_mm_body(x_ref, g_ref, w_ref, o_ref, h_ref):
    @pl.when(pl.program_id(1) == 0)
    def _():
        x = x_ref[...]
        ms = jnp.mean(x * x, axis=-1, keepdims=True)
        h_ref[...] = (x * lax.rsqrt(ms + EPS) * g_ref[...]).astype(BF16)

    o_ref[...] = jnp.dot(h_ref[...], w_ref[...], preferred_element_type=F32).astype(o_ref.dtype)


def _rms_matmul(x, g, w, out_dtype, name):
    T, K = x.shape
    N = w.shape[1]
    tm, tn = _tile(T, 1024), _tile(N, 1024)
    return pl.pallas_call(
        _rms_mm_body,
        out_shape=jax.ShapeDtypeStruct((T, N), out_dtype),
        grid=(T // tm, N // tn),
        in_specs=[pl.BlockSpec((tm, K), lambda i, j: (i, 0)),
                  pl.BlockSpec((1, K), lambda i, j: (0, 0)),
                  pl.BlockSpec((K, tn), lambda i, j: (0, j))],
        out_specs=pl.BlockSpec((tm, tn), lambda i, j: (i, j)),
        scratch_shapes=[pltpu.VMEM((tm, K), BF16)],
        compiler_params=_cparams(("parallel", "arbitrary")),
        name=name,
    )(x, g.reshape(1, K), w)


def _rep_lanes(x, n):
    return jnp.concatenate([x] * n, axis=1)


def _softmax_step(s, m_ref, l_ref, acc_ref, v):
    n_rep = s.shape[1] // LANES
    m_prev = m_ref[...]
    m_new = jnp.maximum(m_prev, jnp.max(s, axis=-1, keepdims=True))
    alpha = jnp.exp2(m_prev - m_new)
    p = jnp.exp2(s - _rep_lanes(m_new, n_rep))
    p_sum = p[:, :LANES]
    for c in range(1, n_rep):
        p_sum = p_sum + p[:, c * LANES:(c + 1) * LANES]
    l_ref[...] = alpha * l_ref[...] + p_sum
    acc_ref[...] = alpha * acc_ref[...] + jnp.dot(p.astype(BF16), v, preferred_element_type=F32)
    m_ref[...] = m_new


def _softmax_result(l_ref, acc_ref):
    return acc_ref[...] / jnp.sum(l_ref[...], axis=-1, keepdims=True)


def _sweep_far_blocks(n_far, step):
    def pair(p, carry):
        step(2 * p)
        step(2 * p + 1)
        return carry
    lax.fori_loop(0, n_far // 2, pair, 0)

    @pl.when(n_far % 2 == 1)
    def _():
        step(n_far - 1)


def _softmax_init(m_ref, l_ref, acc_ref):
    m_ref[...] = jnp.full(m_ref.shape, NEG_BIG, F32)
    l_ref[...] = jnp.zeros(l_ref.shape, F32)
    acc_ref[...] = jnp.zeros(acc_ref.shape, F32)


def _t5_tiles_body(rb_ref, o_ref, *, tq):
    h = pl.program_id(0)
    row = lax.broadcasted_iota(I32, (tq, tq), 0)
    col = lax.broadcasted_iota(I32, (tq, tq), 1)
    max_exact = REL_BUCKETS // 2
    far = rb_ref[(REL_BUCKETS - 1) * DIFF_HEADS + h]
    for which in range(2):
        d = row - col + which * tq
        n = jnp.maximum(d, 0)
        nf = jnp.maximum(n, 1).astype(F32)
        large = max_exact + (jnp.log(nf / max_exact) / math.log(REL_MAX_DIST / max_exact)
                             * (REL_BUCKETS - max_exact)).astype(I32)
        large = jnp.minimum(large, REL_BUCKETS - 1)
        bucket = jnp.where(n < max_exact, n, large)
        val = jnp.zeros((tq, tq), F32)
        for b in range(REL_BUCKETS - 1):
            val = jnp.where(bucket == b, (rb_ref[b * DIFF_HEADS + h] - far) * LOG2_E, val)
        o_ref[0, which] = jnp.where(d >= 0, val, NEG_BIG)


def _t5_bias_tiles(rel_bias, tq):
    assert tq >= REL_MAX_DIST
    return pl.pallas_call(
        functools.partial(_t5_tiles_body, tq=tq),
        out_shape=jax.ShapeDtypeStruct((DIFF_HEADS, 2, tq, tq), F32),
        grid=(DIFF_HEADS,),
        in_specs=[pl.BlockSpec(memory_space=pltpu.SMEM)],
        out_specs=pl.BlockSpec((1, 2, tq, tq), lambda h: (h, 0, 0, 0)),
        compiler_params=_cparams(("parallel",)),
        name="t5_bias_tiles",
    )(rel_bias.reshape(-1))


def _diff_attn_body(lam_ref, qg_ref, kg_ref, sg_ref, bias_ref, q_ref, k_ref, v_ref, o_ref,
                    kn_s, m1, l1, a1, m2, l2, a2, *, tq, seq):
    qi = pl.program_id(2)
    lo = lax.broadcasted_iota(I32, (1, HEAD_W), 1) < DIFF_QK_DIM

    def half_rms(x, g):
        x2 = x * x
        s_lo = jnp.sum(jnp.where(lo, x2, 0.0), axis=-1, keepdims=True)
        s_hi = jnp.sum(jnp.where(lo, 0.0, x2), axis=-1, keepdims=True)
        r = lax.rsqrt(jnp.where(lo, s_lo, s_hi) * (1.0 / DIFF_QK_DIM) + EPS)
        return x * r * g

    @pl.when(qi == 0)
    def _():
        def norm_chunk(c, carry):
            rows = pl.ds(pl.multiple_of(c * tq, tq), tq)
            kn_s[rows, :] = half_rms(k_ref[rows, :].astype(F32), kg_ref[...]).astype(BF16)
            return carry
        lax.fori_loop(0, seq // tq, norm_chunk, 0)

    qn = half_rms(q_ref[...].astype(F32), qg_ref[...]) * (DIFF_QK_DIM ** -0.5 * LOG2_E)
    q1 = jnp.where(lo, qn, 0.0).astype(BF16)
    q2 = jnp.where(lo, 0.0, qn).astype(BF16)

    _softmax_init(m1, l1, a1)
    _softmax_init(m2, l2, a2)

    def step(j, bias):
        rows = pl.ds(pl.multiple_of(j * tq, tq), tq)
        kb = kn_s[rows, :]
        vb = v_ref[rows, :]
        s1 = lax.dot_general(q1, kb, _NT, preferred_element_type=F32)
        s2 = lax.dot_general(q2, kb, _NT, preferred_element_type=F32)
        if bias is not None:
            s1 = s1 + bias
            s2 = s2 + bias
        _softmax_step(s1, m1, l1, a1, vb)
        _softmax_step(s2, m2, l2, a2, vb)

    _sweep_far_blocks(jnp.maximum(qi - 1, 0), lambda j: step(j, None))

    @pl.when(qi >= 1)
    def _():
        step(qi - 1, bias_ref[0, 1])

    step(qi, bias_ref[0, 0])

    lp = lam_ref[...]
    lam_init = lp[4:5, 0:1]
    lam = (jnp.exp(jnp.sum(lp[0:1] * lp[1:2], axis=-1, keepdims=True))
           - jnp.exp(jnp.sum(lp[2:3] * lp[3:4], axis=-1, keepdims=True)) + lam_init)
    o = _softmax_result(l1, a1) - lam * _softmax_result(l2, a2)
    ms = jnp.mean(o * o, axis=-1, keepdims=True)
    o = o * lax.rsqrt(ms + EPS) * sg_ref[...] * (1.0 - lam_init)
    o_ref[...] = o.astype(BF16)


def _diff_attention(proj, lam_tab, q_gain, k_gain, sub_gain, bias_tiles, batch, seq, tq):
    T = proj.shape[0]
    nq = seq // tq
    H = DIFF_HEADS
    small = lambda b, h, i: (0, 0)
    return pl.pallas_call(
        functools.partial(_diff_attn_body, tq=tq, seq=seq),
        out_shape=jax.ShapeDtypeStruct((T, BRANCH_WIDTH), BF16),
        grid=(batch, H, nq),
        in_specs=[pl.BlockSpec((8, DIFF_QK_DIM), small),
                  pl.BlockSpec((1, HEAD_W), small),
                  pl.BlockSpec((1, HEAD_W), small),
                  pl.BlockSpec((1, HEAD_W), small),
                  pl.BlockSpec((1, 2, tq, tq), lambda b, h, i: (h, 0, 0, 0)),
                  pl.BlockSpec((tq, HEAD_W), lambda b, h, i: (b * nq + i, h)),
                  pl.BlockSpec((seq, HEAD_W), lambda b, h, i: (b, H + h)),
                  pl.BlockSpec((seq, HEAD_W), lambda b, h, i: (b, 2 * H + h))],
        out_specs=pl.BlockSpec((tq, HEAD_W), lambda b, h, i: (b * nq + i, h)),
        scratch_shapes=[pltpu.VMEM((seq, HEAD_W), BF16),
                        pltpu.VMEM((tq, LANES), F32), pltpu.VMEM((tq, LANES), F32), pltpu.VMEM((tq, HEAD_W), F32),
                        pltpu.VMEM((tq, LANES), F32), pltpu.VMEM((tq, LANES), F32), pltpu.VMEM((tq, HEAD_W), F32)],
        compiler_params=_cparams(("parallel", "parallel", "arbitrary")),
        name="diff_attn",
    )(lam_tab, q_gain, k_gain, sub_gain, bias_tiles, proj, proj, proj)


def _sb_attn_body(q_ref, k_ref, v_ref, o_ref, c_s, acc_s, *, tq):
    qi = pl.program_id(2)
    q = q_ref[...]
    row = lax.broadcasted_iota(I32, (tq, tq), 0)
    col = lax.broadcasted_iota(I32, (tq, tq), 1)
    tri = jnp.where(row >= col, 1.0, 0.0).astype(BF16)
    tri2 = jnp.concatenate([tri, tri], axis=0)
    strict = col < row
    n_rep = tq // LANES

    c_s[...] = jnp.zeros(c_s.shape, F32)
    acc_s[...] = jnp.zeros(acc_s.shape, F32)

    def block(j, masked):
        rows = pl.ds(pl.multiple_of(j * tq, tq), tq)
        z = lax.dot_general(q, k_ref[rows, :], _NT, preferred_element_type=F32) * (SB_HEAD_DIM ** -0.5 * LOG2_E)
        log_fail = -(jnp.maximum(z, 0.0) + jnp.log2(1.0 + jnp.exp2(-jnp.abs(z))))
        if masked:
            log_fail = jnp.where(strict, log_fail, 0.0)
        hi = log_fail.astype(BF16)
        lo = (log_fail - hi.astype(F32)).astype(BF16)
        cum = jnp.dot(jnp.concatenate([hi, lo], axis=1), tri2, preferred_element_type=F32)
        c = c_s[...]
        a = jnp.exp2(z + cum + _rep_lanes(c, n_rep))
        if masked:
            a = jnp.where(strict, a, 0.0)
        acc_s[...] += jnp.dot(a.astype(BF16), v_ref[rows, :], preferred_element_type=F32)
        c_new = c + jnp.sum(log_fail, axis=-1, keepdims=True)
        c_s[...] = c_new
        return jnp.max(c_new)

    c_max = block(qi, True)

    def cond(carry):
        j, c_max = carry
        return jnp.logical_and(j >= 0, c_max > SB_SKIP_LOG)

    def body(carry):
        j, _ = carry
        return j - 1, block(j, False)

    lax.while_loop(cond, body, (qi - 1, c_max))
    o_ref[...] = acc_s[...].astype(BF16)


def _sb_attention(proj, batch, seq, tq):
    T = proj.shape[0]
    nq = seq // tq
    H = SB_HEADS
    base = 3 * DIFF_HEADS
    return pl.pallas_call(
        functools.partial(_sb_attn_body, tq=tq),
        out_shape=jax.ShapeDtypeStruct((T, BRANCH_WIDTH), BF16),
        grid=(batch, H, nq),
        in_specs=[pl.BlockSpec((tq, HEAD_W), lambda b, h, i: (b * nq + i, base + h)),
                  pl.BlockSpec((seq, HEAD_W), lambda b, h, i: (b, base + H + h)),
                  pl.BlockSpec((seq, HEAD_W), lambda b, h, i: (b, base + 2 * H + h))],
        out_specs=pl.BlockSpec((tq, HEAD_W), lambda b, h, i: (b * nq + i, h)),
        scratch_shapes=[pltpu.VMEM((tq, LANES), F32), pltpu.VMEM((tq, HEAD_W), F32)],
        compiler_params=_cparams(("parallel", "parallel", "arbitrary")),
        name="sb_attn",
    )(proj, proj, proj)


def _mla_prep_body(pm_ref, qa_ref, kva_ref, wuq_ref, wukv_ref, qg_ref, kg_ref,
                   cos_ref, sina_ref, sinb_ref, q_ref, k_ref, v_ref):
    x = pm_ref[...].astype(F32)

    def rms(t, g):
        return t * lax.rsqrt(jnp.mean(t * t, axis=-1, keepdims=True) + EPS) * g

    cq = rms(x[:, :MLA_Q_RANK], qa_ref[...]).astype(BF16)
    ckv = rms(x[:, MLA_Q_RANK:MLA_Q_RANK + MLA_KV_RANK], kva_ref[...]).astype(BF16)
    kr = x[:, MLA_Q_RANK + MLA_KV_RANK:MLA_Q_RANK + MLA_KV_RANK + LANES]
    q = jnp.dot(cq, wuq_ref[...], preferred_element_type=F32)
    kv = jnp.dot(ckv, wukv_ref[...], preferred_element_type=F32)
    cos, sina, sinb = cos_ref[...], sina_ref[...], sinb_ref[...]
    half = MLA_ROPE_DIM // 2

    def rope(pe):
        return pe * cos + pltpu.roll(pe, LANES - half, 1) * sina + pltpu.roll(pe, half, 1) * sinb

    k_pe = rope(kr)
    k_pe_ss = jnp.sum(k_pe * k_pe, axis=-1, keepdims=True)
    qk_dim = MLA_NOPE_DIM + MLA_ROPE_DIM
    qg, kg = qg_ref[...], kg_ref[...]
    for h in range(MLA_HEADS):
        o = h * MLA_QK_PAD
        q_n = q[:, o:o + MLA_NOPE_DIM]
        q_pe = rope(q[:, o + MLA_NOPE_DIM:o + MLA_QK_PAD])
        ss = jnp.sum(q_n * q_n, axis=-1, keepdims=True) + jnp.sum(q_pe * q_pe, axis=-1, keepdims=True)
        r = lax.rsqrt(ss * (1.0 / qk_dim) + EPS) * (qk_dim ** -0.5 * LOG2_E)
        q_ref[:, o:o + MLA_NOPE_DIM] = (q_n * r * qg[:, :MLA_NOPE_DIM]).astype(BF16)
        q_ref[:, o + MLA_NOPE_DIM:o + MLA_QK_PAD] = (q_pe * r * qg[:, MLA_NOPE_DIM:]).astype(BF16)
        k_n = kv[:, o:o + MLA_NOPE_DIM]
        ssk = jnp.sum(k_n * k_n, axis=-1, keepdims=True) + k_pe_ss
        rk = lax.rsqrt(ssk * (1.0 / qk_dim) + EPS)
        k_ref[:, o:o + MLA_NOPE_DIM] = (k_n * rk * kg[:, :MLA_NOPE_DIM]).astype(BF16)
        k_ref[:, o + MLA_NOPE_DIM:o + MLA_QK_PAD] = (k_pe * rk * kg[:, MLA_NOPE_DIM:]).astype(BF16)
        v_ref[:, h * MLA_V_DIM:(h + 1) * MLA_V_DIM] = kv[:, o + MLA_NOPE_DIM:o + MLA_QK_PAD].astype(BF16)


def _mla_prep(proj, mla_blk, qa_g, kva_g, w_uq, w_ukv, q_g, k_g, cos_t, sina_t, sinb_t, seq):
    T = proj.shape[0]
    tm = _tile(seq, 512)
    ns = seq // tm
    W = MLA_HEADS * MLA_QK_PAD
    const = lambda i: (0, 0)
    pos = lambda i: (i % ns, 0)
    return pl.pallas_call(
        _mla_prep_body,
        out_shape=(jax.ShapeDtypeStruct((T, W), BF16), jax.ShapeDtypeStruct((T, W), BF16),
                   jax.ShapeDtypeStruct((T, MLA_HEADS * MLA_V_DIM), BF16)),
        grid=(T // tm,),
        in_specs=[pl.BlockSpec((tm, MLA_IN_W), lambda i: (i, mla_blk)),
                  pl.BlockSpec((1, MLA_Q_RANK), const),
                  pl.BlockSpec((1, MLA_KV_RANK), const),
                  pl.BlockSpec((MLA_Q_RANK, W), const),
                  pl.BlockSpec((MLA_KV_RANK, W), const),
                  pl.BlockSpec((1, MLA_QK_PAD), const),
                  pl.BlockSpec((1, MLA_QK_PAD), const),
                  pl.BlockSpec((tm, LANES), pos),
                  pl.BlockSpec((tm, LANES), pos),
                  pl.BlockSpec((tm, LANES), pos)],
        out_specs=(pl.BlockSpec((tm, W), lambda i: (i, 0)),
                   pl.BlockSpec((tm, W), lambda i: (i, 0)),
                   pl.BlockSpec((tm, MLA_HEADS * MLA_V_DIM), lambda i: (i, 0))),
        compiler_params=_cparams(("parallel",)),
        name="mla_prep",
    )(proj, qa_g, kva_g, w_uq, w_ukv, q_g, k_g, cos_t, sina_t, sinb_t)


def _mla_attn_body(q_ref, k_ref, v_ref, o_ref, m, l, acc, *, tq):
    qi = pl.program_id(2)
    q = q_ref[...]
    _softmax_init(m, l, acc)

    def step(j, masked):
        rows = pl.ds(pl.multiple_of(j * tq, tq), tq)
        s = lax.dot_general(q, k_ref[rows, :], _NT, preferred_element_type=F32)
        if masked:
            row = lax.broadcasted_iota(I32, (tq, tq), 0)
            col = lax.broadcasted_iota(I32, (tq, tq), 1)
            s = jnp.where(col <= row, s, NEG_BIG)
        _softmax_step(s, m, l, acc, v_ref[rows, :])

    _sweep_far_blocks(qi, lambda j: step(j, False))
    step(qi, True)
    o_ref[...] = _softmax_result(l, acc).astype(BF16)


def _mla_attention(q, k, v, batch, seq, tq):
    T = q.shape[0]
    nq = seq // tq
    return pl.pallas_call(
        functools.partial(_mla_attn_body, tq=tq),
        out_shape=jax.ShapeDtypeStruct((T, BRANCH_WIDTH), BF16),
        grid=(batch, MLA_HEADS, nq),
        in_specs=[pl.BlockSpec((tq, MLA_QK_PAD), lambda b, h, i: (b * nq + i, h)),
                  pl.BlockSpec((seq, MLA_QK_PAD), lambda b, h, i: (b, h)),
                  pl.BlockSpec((seq, MLA_V_DIM), lambda b, h, i: (b, h))],
        out_specs=pl.BlockSpec((tq, MLA_V_DIM), lambda b, h, i: (b * nq + i, h)),
        scratch_shapes=[pltpu.VMEM((tq, LANES), F32), pltpu.VMEM((tq, LANES), F32),
                        pltpu.VMEM((tq, MLA_V_DIM), F32)],
        compiler_params=_cparams(("parallel", "parallel", "arbitrary")),
        name="mla_attn",
    )(q, k, v)


def _merge_body(oa_ref, ob_ref, oc_ref, wb_ref, ga_ref, gb_ref, gc_ref, bg_ref, o_ref):
    acc = None
    for b, (o_r, g_r) in enumerate(((oa_ref, ga_ref), (ob_ref, gb_ref), (oc_ref, gc_ref))):
        gate = jax.nn.sigmoid(g_r[...].astype(F32) + bg_ref[b:b + 1, :])
        t = gate * jnp.dot(o_r[...], wb_ref[b], preferred_element_type=F32)
        acc = t if acc is None else acc + t
    o_ref[...] = acc.astype(BF16)


def _branch_merge(o_a, o_b, o_c, w_branch, proj, b_gate, d_model):
    T = o_a.shape[0]
    tm, tn = _tile(T, 1024), _tile(d_model, 512)
    assert QKV_W % tn == 0
    g0 = QKV_W // tn
    nj = d_model // tn
    o_spec = pl.BlockSpec((tm, BRANCH_WIDTH), lambda i, j: (i, 0))
    gate_spec = lambda b: pl.BlockSpec((tm, tn), lambda i, j: (i, g0 + b * nj + j))
    return pl.pallas_call(
        _merge_body,
        out_shape=jax.ShapeDtypeStruct((T, d_model), BF16),
        grid=(T // tm, nj),
        in_specs=[o_spec, o_spec, o_spec,
                  pl.BlockSpec((N_BRANCH, BRANCH_WIDTH, tn), lambda i, j: (0, 0, j)),
                  gate_spec(0), gate_spec(1), gate_spec(2),
                  pl.BlockSpec((N_BRANCH, tn), lambda i, j: (0, j))],
        out_specs=pl.BlockSpec((tm, tn), lambda i, j: (i, j)),
        compiler_params=_cparams(("parallel", "arbitrary")),
        name="branch_merge",
    )(o_a, o_b, o_c, w_branch, proj, proj, proj, b_gate)


def _out_proj_body(a_ref, w_ref, r_ref, o_ref):
    o_ref[...] = r_ref[...] + jnp.dot(a_ref[...], w_ref[...], preferred_element_type=F32)


def _out_proj(merged, w_out, x):
    T, K = merged.shape
    N = w_out.shape[1]
    tm, tn = _tile(T, 1024), _tile(N, 1024)
    return pl.pallas_call(
        _out_proj_body,
        out_shape=jax.ShapeDtypeStruct((T, N), F32),
        grid=(T // tm, N // tn),
        in_specs=[pl.BlockSpec((tm, K), lambda i, j: (i, 0)),
                  pl.BlockSpec((K, tn), lambda i, j: (0, j)),
                  pl.BlockSpec((tm, tn), lambda i, j: (i, j))],
        out_specs=pl.BlockSpec((tm, tn), lambda i, j: (i, j)),
        compiler_params=_cparams(("parallel", "arbitrary")),
        name="out_proj",
    )(merged, w_out, x)


def _route_body(x_ref, g_ref, wr_ref, br_ref, h_ref, meta_ref, cnt_ref, carry_s, *, tm):
    i = pl.program_id(0)
    d_model = x_ref.shape[1]

    @pl.when(i == 0)
    def _():
        carry_s[...] = jnp.zeros(carry_s.shape, F32)

    x = x_ref[...]
    h = x * lax.rsqrt(jnp.mean(x * x, axis=-1, keepdims=True) + EPS) * g_ref[...]
    h_ref[:, :d_model] = h
    logits = jnp.dot(h, wr_ref[...], preferred_element_type=F32,
                     precision=lax.Precision.HIGHEST) + br_ref[...]
    lane = lax.broadcasted_iota(I32, (tm, LANES), 1)
    big = jnp.int32(LANES)

    gl = jnp.where(lane < N_GROUPS, logits, -jnp.inf)
    g_max = jnp.max(gl, axis=-1, keepdims=True)
    g_idx = jnp.min(jnp.where(gl == g_max, lane, big), axis=-1, keepdims=True)
    g_val = 1.0 / jnp.sum(jnp.exp(gl - g_max), axis=-1, keepdims=True)

    e_lo = N_GROUPS + g_idx * EXPERTS_PER_GROUP
    el = jnp.where(jnp.logical_and(lane >= e_lo, lane < e_lo + EXPERTS_PER_GROUP), logits, -jnp.inf)
    e1 = jnp.max(el, axis=-1, keepdims=True)
    i1 = jnp.min(jnp.where(el == e1, lane, big), axis=-1, keepdims=True)
    el2 = jnp.where(lane == i1, -jnp.inf, el)
    e2 = jnp.max(el2, axis=-1, keepdims=True)
    i2 = jnp.min(jnp.where(el2 == e2, lane, big), axis=-1, keepdims=True)
    t = jnp.exp(e2 - e1)
    w1 = g_val / (1.0 + t)
    w2 = g_val * t / (1.0 + t)
    h_ref[:, d_model:] = jnp.where(lane == i1, w1, 0.0) + jnp.where(lane == i2, w2, 0.0)

    onehot = jnp.where(lane == g_idx, 1.0, 0.0)
    r = lax.broadcasted_iota(I32, (tm, tm), 0)
    c = lax.broadcasted_iota(I32, (tm, tm), 1)
    before = jnp.where(c < r, 1.0, 0.0).astype(BF16)
    prefix = jnp.dot(before, onehot.astype(BF16), preferred_element_type=F32) + carry_s[...]
    rank = jnp.sum(jnp.where(lane == g_idx, prefix, 0.0), axis=-1, keepdims=True)
    carry_s[...] += jnp.sum(onehot, axis=0, keepdims=True)
    meta_ref[...] = jnp.where(lane == 0, g_idx, jnp.where(lane == 1, rank.astype(I32), 0))
    cnt_ref[...] = jnp.broadcast_to(carry_s[...], cnt_ref.shape).astype(I32)


def _moe_route(x, g, w_r, b_r):
    T, D = x.shape
    tm = _tile(T, 512)
    return pl.pallas_call(
        functools.partial(_route_body, tm=tm),
        out_shape=(jax.ShapeDtypeStruct((T, D + LANES), F32),
                   jax.ShapeDtypeStruct((T, LANES), I32), jax.ShapeDtypeStruct((8, LANES), I32)),
        grid=(T // tm,),
        in_specs=[pl.BlockSpec((tm, D), lambda i: (i, 0)),
                  pl.BlockSpec((1, D), lambda i: (0, 0)),
                  pl.BlockSpec((D, LANES), lambda i: (0, 0)),
                  pl.BlockSpec((1, LANES), lambda i: (0, 0))],
        out_specs=(pl.BlockSpec((tm, D + LANES), lambda i: (i, 0)),
                   pl.BlockSpec((tm, LANES), lambda i: (i, 0)),
                   pl.BlockSpec((8, LANES), lambda i: (0, 0))),
        scratch_shapes=[pltpu.VMEM((1, LANES), F32)],
        compiler_params=_cparams(("arbitrary",)),
        name="moe_route",
    )(x, g.reshape(1, D), w_r, b_r)


def _invert_body(dest_ref, src_ref, *, n_tok, n_rows):
    def clear(p, carry):
        src_ref[p] = 0
        return carry
    lax.fori_loop(0, n_rows, clear, 0)

    def put(t, carry):
        src_ref[dest_ref[t]] = t
        return carry
    lax.fori_loop(0, n_tok, put, 0)


def _moe_invert(dest, t_pad):
    return pl.pallas_call(
        functools.partial(_invert_body, n_tok=dest.shape[0], n_rows=t_pad),
        out_shape=jax.ShapeDtypeStruct((t_pad,), I32),
        in_specs=[pl.BlockSpec(memory_space=pltpu.SMEM)],
        out_specs=pl.BlockSpec(memory_space=pltpu.SMEM),
        name="moe_invert",
    )(dest)


def _gather_rows_into(idx_ref, table_ref, dst_ref, sem, n_rows):
    base = pl.program_id(0) * n_rows

    def row_copy(r):
        return pltpu.make_async_copy(table_ref.at[pl.ds(idx_ref[base + r], 1)], dst_ref.at[pl.ds(r, 1)], sem.at[0])

    def issue(r, carry):
        row_copy(r).start()
        return carry
    lax.fori_loop(0, n_rows, issue, 0)

    def drain(r, carry):
        row_copy(r).wait()
        return carry
    lax.fori_loop(0, n_rows, drain, 0)


def _dispatch_body(src_ref, h_ref, o_ref, sem, *, tm):
    _gather_rows_into(src_ref, h_ref, o_ref, sem, tm)


def _moe_dispatch(src, h):
    t_pad = src.shape[0]
    W = h.shape[1]
    tm = _tile(t_pad, 256)
    return pl.pallas_call(
        functools.partial(_dispatch_body, tm=tm),
        out_shape=jax.ShapeDtypeStruct((t_pad, W), F32),
        grid_spec=pltpu.PrefetchScalarGridSpec(
            num_scalar_prefetch=1, grid=(t_pad // tm,),
            in_specs=[pl.BlockSpec(memory_space=pl.ANY)],
            out_specs=pl.BlockSpec((tm, W), lambda i, s: (i, 0)),
            scratch_shapes=[pltpu.SemaphoreType.DMA((1,))]),
        compiler_params=_cparams(("arbitrary",)),
        name="moe_dispatch",
    )(src, h)


def _experts_body(tg_ref, nt_ref, xs_ref, wgu_ref, wd_ref, y_ref, *, d_expert):
    i, e = pl.program_id(0), pl.program_id(1)
    valid = i < nt_ref[0]
    d_model = y_ref.shape[1]

    @pl.when(valid)
    def _():
        x = xs_ref[:, :d_model].astype(BF16)
        gu = jnp.dot(x, wgu_ref[0], preferred_element_type=F32)
        comb = xs_ref[:, d_model:]
        lane = lax.broadcasted_iota(I32, comb.shape, 1)
        sel = N_GROUPS + tg_ref[i] * EXPERTS_PER_GROUP + e
        c = jnp.sum(jnp.where(lane == sel, comb, 0.0), axis=-1, keepdims=True)
        a = (jax.nn.silu(gu[:, :d_expert]) * gu[:, d_expert:] * c).astype(BF16)
        y = jnp.dot(a, wd_ref[0], preferred_element_type=F32)

        @pl.when(e == 0)
        def _():
            y_ref[...] = y

        @pl.when(e > 0)
        def _():
            y_ref[...] += y

    @pl.when(jnp.logical_and(jnp.logical_not(valid), e == 0))
    def _():
        y_ref[...] = jnp.zeros(y_ref.shape, F32)


def _moe_experts(tile_group, n_tiles, xs, w_gu, w_down, tg):
    t_pad = xs.shape[0]
    d_expert, D = w_down.shape[1:]
    expert = lambda i, e, tgr, ntr: (tgr[i] * EXPERTS_PER_GROUP + e, 0, 0)
    rows = lambda i, e, tgr, ntr: (i, 0)
    return pl.pallas_call(
        functools.partial(_experts_body, d_expert=d_expert),
        out_shape=jax.ShapeDtypeStruct((t_pad, D), F32),
        grid_spec=pltpu.PrefetchScalarGridSpec(
            num_scalar_prefetch=2, grid=(t_pad // tg, EXPERTS_PER_GROUP),
            in_specs=[pl.BlockSpec((tg, D + LANES), rows),
                      pl.BlockSpec((1, D, 2 * d_expert), expert),
                      pl.BlockSpec((1, d_expert, D), expert)],
            out_specs=pl.BlockSpec((tg, D), rows)),
        compiler_params=_cparams(("parallel", "arbitrary")),
        name="moe_experts",
    )(tile_group, n_tiles, xs, w_gu, w_down)


def _combine_body(dest_ref, x_ref, ys_ref, o_ref, buf, sem, *, tm):
    _gather_rows_into(dest_ref, ys_ref, buf, sem, tm)
    o_ref[...] = x_ref[...] + buf[...]


def _moe_combine(dest, x, ys):
    T, D = x.shape
    tm = _tile(T, 256)
    return pl.pallas_call(
        functools.partial(_combine_body, tm=tm),
        out_shape=jax.ShapeDtypeStruct((T, D), F32),
        grid_spec=pltpu.PrefetchScalarGridSpec(
            num_scalar_prefetch=1, grid=(T // tm,),
            in_specs=[pl.BlockSpec((tm, D), lambda i, d: (i, 0)),
                      pl.BlockSpec(memory_space=pl.ANY)],
            out_specs=pl.BlockSpec((tm, D), lambda i, d: (i, 0)),
            scratch_shapes=[pltpu.VMEM((tm, D), F32), pltpu.SemaphoreType.DMA((1,))]),
        compiler_params=_cparams(("arbitrary",)),
        name="moe_combine",
    )(dest, x, ys)


def _hier_moe(x, ln_g, w_r, b_r, w_gu, w_down):
    T, D = x.shape
    tg = _tile(T, 512)
    h, meta, counts = _moe_route(x, ln_g, w_r, b_r)
    cnt = counts[0, :N_GROUPS]
    seg_tiles = (cnt + tg - 1) // tg
    tile_end = jnp.cumsum(seg_tiles)
    seg_off = (tile_end - seg_tiles) * tg
    dest = seg_off[meta[:, 0]] + meta[:, 1]
    n_tiles_max = T // tg + N_GROUPS
    n_tiles = tile_end[-1]
    tile_ids = jnp.arange(n_tiles_max, dtype=I32)
    tile_group = jnp.searchsorted(tile_end, jnp.minimum(tile_ids, n_tiles - 1), side="right").astype(I32)
    dest = dest.astype(I32)
    xs = _moe_dispatch(_moe_invert(dest, n_tiles_max * tg), h)
    ys = _moe_experts(tile_group, n_tiles.reshape(1).astype(I32), xs, w_gu, w_down, tg)
    return _moe_combine(dest, x, ys)


def _layout_w_in(w_in, d_model):
    mla_w = MLA_Q_RANK + MLA_KV_RANK + MLA_ROPE_DIM
    gate_w = N_BRANCH * d_model
    mla_off = _round_up(QKV_W + gate_w, MLA_IN_W)
    n_pad = _round_up(mla_off + MLA_IN_W, 1024)
    w = jnp.concatenate([
        w_in[:, :QKV_W],
        w_in[:, QKV_W + mla_w:],
        jnp.zeros((d_model, mla_off - QKV_W - gate_w), w_in.dtype),
        w_in[:, QKV_W:QKV_W + mla_w],
        jnp.zeros((d_model, n_pad - mla_off - mla_w), w_in.dtype)], axis=1)
    return w.astype(BF16), mla_off // MLA_IN_W


def _layout_w_uq(w_uq):
    w = w_uq.reshape(MLA_Q_RANK, MLA_HEADS, MLA_NOPE_DIM + MLA_ROPE_DIM)
    w = jnp.pad(w, ((0, 0), (0, 0), (0, MLA_QK_PAD - MLA_NOPE_DIM - MLA_ROPE_DIM)))
    return w.reshape(MLA_Q_RANK, MLA_HEADS * MLA_QK_PAD).astype(BF16)


def _pad_gain(g):
    return jnp.pad(g, (0, MLA_QK_PAD - g.shape[0])).reshape(1, MLA_QK_PAD)


def _rope_tables(seq):
    pos = jnp.arange(seq, dtype=F32)
    inv_freq = 1.0 / (ROPE_THETA ** (jnp.arange(0, MLA_ROPE_DIM, 2, dtype=F32) / MLA_ROPE_DIM))
    ang = pos[:, None] * inv_freq[None, :]
    cos, sin = jnp.cos(ang), jnp.sin(ang)
    z = jnp.zeros_like(cos)
    return (jnp.concatenate([cos, cos, z, z], axis=1),
            jnp.concatenate([-sin, z, z, z], axis=1),
            jnp.concatenate([z, sin, z, z], axis=1))


def kernel(x, rel_bias, ln1_g, w_in, diff_q_norm, diff_k_norm, diff_lambda, diff_subln_g, mla_q_a_norm, mla_w_uq, mla_kv_a_norm, mla_w_ukv, mla_q_norm, mla_k_norm, w_branch, b_gate, w_out, ln2_g, router_group_w, router_group_b, router_expert_w, router_expert_b, expert_w_gu, expert_w_down):
    B, S, D = x.shape
    depth = w_in.shape[0]
    T = B * S
    tq = _tile(S, 512)
    tq_sb = _tile(S, 512)
    cos_t, sina_t, sinb_t = _rope_tables(S)
    bias_tiles = _t5_bias_tiles(rel_bias, tq)
    xt = x.reshape(T, D)
    for l in range(depth):
        w_in_l, mla_blk = _layout_w_in(w_in[l], D)
        proj = _rms_matmul(xt, ln1_g[l], w_in_l, BF16, "rms_in_proj")

        lam_init = 0.8 - 0.6 * math.exp(-0.3 * l)
        lam_tab = jnp.concatenate([diff_lambda[l], jnp.full((4, DIFF_QK_DIM), lam_init, F32)], axis=0)
        o_a = _diff_attention(proj, lam_tab,
                              jnp.tile(diff_q_norm[l], 2).reshape(1, HEAD_W),
                              jnp.tile(diff_k_norm[l], 2).reshape(1, HEAD_W),
                              diff_subln_g[l].reshape(1, HEAD_W), bias_tiles, B, S, tq)
        o_b = _sb_attention(proj, B, S, tq_sb)
        q_c, k_c, v_c = _mla_prep(proj, mla_blk, mla_q_a_norm[l].reshape(1, -1), mla_kv_a_norm[l].reshape(1, -1),
                                  _layout_w_uq(mla_w_uq[l]), mla_w_ukv[l].astype(BF16),
                                  _pad_gain(mla_q_norm[l]), _pad_gain(mla_k_norm[l]),
                                  cos_t, sina_t, sinb_t, S)
        o_c = _mla_attention(q_c, k_c, v_c, B, S, tq)

        merged = _branch_merge(o_a, o_b, o_c, w_branch[l].astype(BF16), proj,
                               b_gate[l].reshape(N_BRANCH, D), D)
        xt = _out_proj(merged, w_out[l].astype(BF16), xt)

        w_r = jnp.concatenate([router_group_w[l], router_expert_w[l],
                               jnp.zeros((D, LANES - N_GROUPS - N_EXPERTS), F32)], axis=1)
        b_r = jnp.concatenate([router_group_b[l], router_expert_b[l],
                               jnp.zeros((LANES - N_GROUPS - N_EXPERTS,), F32)]).reshape(1, LANES)
        xt = _hier_moe(xt, ln2_g[l], w_r, b_r, expert_w_gu[l].astype(BF16), expert_w_down[l].astype(BF16))
    return xt.reshape(B, S, D)
```

```python
import functools
import math

import jax
import jax.numpy as jnp
from jax import lax
from jax.experimental import pallas as pl
from jax.experimental.pallas import tpu as pltpu

F32 = jnp.float32
BF16 = jnp.bfloat16
I32 = jnp.int32

EPS = 1e-6
LANES = 128
HEAD_W = 128
DIFF_HEADS = 8
DIFF_QK_DIM = 64
SB_HEADS = 8
SB_HEAD_DIM = 128
MLA_HEADS = 8
MLA_Q_RANK = 512
MLA_KV_RANK = 256
MLA_NOPE_DIM = 128
MLA_ROPE_DIM = 64
MLA_V_DIM = 128
MLA_QK_PAD = 256
ROPE_THETA = 10000.0
REL_BUCKETS = 32
REL_MAX_DIST = 128
N_GROUPS = 8
EXPERTS_PER_GROUP = 4
N_EXPERTS = N_GROUPS * EXPERTS_PER_GROUP
N_BRANCH = 3
BRANCH_WIDTH = 1024
QKV_W = 6 * BRANCH_WIDTH
MLA_IN_W = 1024
NEG_BIG = -1e30
LOG2_E = math.log2(math.e)
SB_SKIP_LOG = -100.0 * LOG2_E
VMEM_LIMIT = 56 * 1024 * 1024

_NT = (((1,), (1,)), ((), ()))


def _cparams(sem, vmem=VMEM_LIMIT):
    return pltpu.CompilerParams(dimension_semantics=sem, vmem_limit_bytes=vmem)


def _tile(n, target):
    t = min(n, target)
    while n % t:
        t -= 1
    return t


def _round_up(n, m):
    return -(-n // m) * m


def _rms_mm_body(x_ref, g_ref, w_ref, o_ref, h_ref):
    @pl.when(pl.program_id(1) == 0)
    def _():
        x = x_ref[...]
        ms = jnp.mean(x * x, axis=-1, keepdims=True)
        h_ref[...] = (x * lax.rsqrt(ms + EPS) * g_ref[...]).astype(BF16)

    o_ref[...] = jnp.dot(h_ref[...], w_ref[0], preferred_element_type=F32).astype(o_ref.dtype)


def _rms_matmul(x, g, w_all, layer, out_dtype, name):
    T, K = x.shape
    N = w_all.shape[2]
    tm, tn = _tile(T, 1024), _tile(N, 1024)
    return pl.pallas_call(
        _rms_mm_body,
        out_shape=jax.ShapeDtypeStruct((T, N), out_dtype),
        grid=(T // tm, N // tn),
        in_specs=[pl.BlockSpec((tm, K), lambda i, j: (i, 0)),
                  pl.BlockSpec((1, K), lambda i, j: (0, 0)),
                  pl.BlockSpec((1, K, tn), lambda i, j: (layer, 0, j))],
        out_specs=pl.BlockSpec((tm, tn), lambda i, j: (i, j)),
        scratch_shapes=[pltpu.VMEM((tm, K), BF16)],
        compiler_params=_cparams(("parallel", "arbitrary")),
        name=name,
    )(x, g.reshape(1, K), w_all)


def _rep_lanes(x, n):
    return jnp.concatenate([x] * n, axis=1)


def _softmax_step(s, m_ref, l_ref, acc_ref, v):
    n_rep = s.shape[1] // LANES
    m_prev = m_ref[...]
    m_new = jnp.maximum(m_prev, jnp.max(s, axis=-1, keepdims=True))
    alpha = jnp.exp2(m_prev - m_new)
    p = jnp.exp2(s - _rep_lanes(m_new, n_rep))
    p_sum = p[:, :LANES]
    for c in range(1, n_rep):
        p_sum = p_sum + p[:, c * LANES:(c + 1) * LANES]
    l_ref[...] = alpha * l_ref[...] + p_sum
    acc_ref[...] = alpha * acc_ref[...] + jnp.dot(p.astype(BF16), v, preferred_element_type=F32)
    m_ref[...] = m_new


def _softmax_result(l_ref, acc_ref):
    return acc_ref[...] / jnp.sum(l_ref[...], axis=-1, keepdims=True)


def _sweep_far_blocks(n_far, scores, update):
    def pair(p, carry):
        s_a = scores(2 * p)
        s_b = scores(2 * p + 1)
        update(2 * p, s_a)
        update(2 * p + 1, s_b)
        return carry
    lax.fori_loop(0, n_far // 2, pair, 0)

    @pl.when(n_far % 2 == 1)
    def _():
        update(n_far - 1, scores(n_far - 1))


def _softmax_init(m_ref, l_ref, acc_ref):
    m_ref[...] = jnp.full(m_ref.shape, NEG_BIG, F32)
    l_ref[...] = jnp.zeros(l_ref.shape, F32)
    acc_ref[...] = jnp.zeros(acc_ref.shape, F32)


def _t5_tiles_body(rb_ref, o_ref, *, tq):
    h = pl.program_id(0)
    row = lax.broadcasted_iota(I32, (tq, tq), 0)
    col = lax.broadcasted_iota(I32, (tq, tq), 1)
    max_exact = REL_BUCKETS // 2
    far = rb_ref[(REL_BUCKETS - 1) * DIFF_HEADS + h]
    for which in range(2):
        d = row - col + which * tq
        n = jnp.maximum(d, 0)
        nf = jnp.maximum(n, 1).astype(F32)
        large = max_exact + (jnp.log(nf / max_exact) / math.log(REL_MAX_DIST / max_exact)
                             * (REL_BUCKETS - max_exact)).astype(I32)
        large = jnp.minimum(large, REL_BUCKETS - 1)
        bucket = jnp.where(n < max_exact, n, large)
        val = jnp.zeros((tq, tq), F32)
        for b in range(REL_BUCKETS - 1):
            val = jnp.where(bucket == b, (rb_ref[b * DIFF_HEADS + h] - far) * LOG2_E, val)
        o_ref[0, which] = jnp.where(d >= 0, val, NEG_BIG)


def _t5_bias_tiles(rel_bias, tq):
    assert tq >= REL_MAX_DIST
    return pl.pallas_call(
        functools.partial(_t5_tiles_body, tq=tq),
        out_shape=jax.ShapeDtypeStruct((DIFF_HEADS, 2, tq, tq), F32),
        grid=(DIFF_HEADS,),
        in_specs=[pl.BlockSpec(memory_space=pltpu.SMEM)],
        out_specs=pl.BlockSpec((1, 2, tq, tq), lambda h: (h, 0, 0, 0)),
        compiler_params=_cparams(("parallel",)),
        name="t5_bias_tiles",
    )(rel_bias.reshape(-1))


def _diff_attn_body(lam_ref, qg_ref, kg_ref, sg_ref, bias_ref, q_ref, k_ref, v_ref, o_ref,
                    kn_s, m1, l1, a1, m2, l2, a2, *, tq, seq):
    qi = pl.program_id(2)
    lo = lax.broadcasted_iota(I32, (1, HEAD_W), 1) < DIFF_QK_DIM

    def half_rms(x, g):
        x2 = x * x
        s_lo = jnp.sum(jnp.where(lo, x2, 0.0), axis=-1, keepdims=True)
        s_hi = jnp.sum(jnp.where(lo, 0.0, x2), axis=-1, keepdims=True)
        r = lax.rsqrt(jnp.where(lo, s_lo, s_hi) * (1.0 / DIFF_QK_DIM) + EPS)
        return x * r * g

    @pl.when(qi == 0)
    def _():
        def norm_chunk(c, carry):
            rows = pl.ds(pl.multiple_of(c * tq, tq), tq)
            kn_s[rows, :] = half_rms(k_ref[rows, :].astype(F32), kg_ref[...]).astype(BF16)
            return carry
        lax.fori_loop(0, seq // tq, norm_chunk, 0)

    qn = half_rms(q_ref[...].astype(F32), qg_ref[...]) * (DIFF_QK_DIM ** -0.5 * LOG2_E)
    q1 = jnp.where(lo, qn, 0.0).astype(BF16)
    q2 = jnp.where(lo, 0.0, qn).astype(BF16)

    _softmax_init(m1, l1, a1)
    _softmax_init(m2, l2, a2)

    def block_rows(j):
        return pl.ds(pl.multiple_of(j * tq, tq), tq)

    def scores(j):
        kb = kn_s[block_rows(j), :]
        return (lax.dot_general(q1, kb, _NT, preferred_element_type=F32),
                lax.dot_general(q2, kb, _NT, preferred_element_type=F32))

    def update(j, s, bias=None):
        s1, s2 = s if bias is None else (s[0] + bias, s[1] + bias)
        vb = v_ref[block_rows(j), :]
        _softmax_step(s1, m1, l1, a1, vb)
        _softmax_step(s2, m2, l2, a2, vb)

    _sweep_far_blocks(jnp.maximum(qi - 1, 0), scores, update)

    @pl.when(qi >= 1)
    def _():
        update(qi - 1, scores(qi - 1), bias_ref[0, 1])

    update(qi, scores(qi), bias_ref[0, 0])

    lp = lam_ref[...]
    lam_init = lp[4:5, 0:1]
    lam = (jnp.exp(jnp.sum(lp[0:1] * lp[1:2], axis=-1, keepdims=True))
           - jnp.exp(jnp.sum(lp[2:3] * lp[3:4], axis=-1, keepdims=True)) + lam_init)
    o = _softmax_result(l1, a1) - lam * _softmax_result(l2, a2)
    ms = jnp.mean(o * o, axis=-1, keepdims=True)
    o = o * lax.rsqrt(ms + EPS) * sg_ref[...] * (1.0 - lam_init)
    o_ref[...] = o.astype(BF16)


def _diff_attention(proj, lam_tab, q_gain, k_gain, sub_gain, bias_tiles, batch, seq, tq):
    T = proj.shape[0]
    nq = seq // tq
    H = DIFF_HEADS
    small = lambda b, h, i: (0, 0)
    return pl.pallas_call(
        functools.partial(_diff_attn_body, tq=tq, seq=seq),
        out_shape=jax.ShapeDtypeStruct((T, BRANCH_WIDTH), BF16),
        grid=(batch, H, nq),
        in_specs=[pl.BlockSpec((8, DIFF_QK_DIM), small),
                  pl.BlockSpec((1, HEAD_W), small),
                  pl.BlockSpec((1, HEAD_W), small),
                  pl.BlockSpec((1, HEAD_W), small),
                  pl.BlockSpec((1, 2, tq, tq), lambda b, h, i: (h, 0, 0, 0)),
                  pl.BlockSpec((tq, HEAD_W), lambda b, h, i: (b * nq + i, h)),
                  pl.BlockSpec((seq, HEAD_W), lambda b, h, i: (b, H + h)),
                  pl.BlockSpec((seq, HEAD_W), lambda b, h, i: (b, 2 * H + h))],
        out_specs=pl.BlockSpec((tq, HEAD_W), lambda b, h, i: (b * nq + i, h)),
        scratch_shapes=[pltpu.VMEM((seq, HEAD_W), BF16),
                        pltpu.VMEM((tq, LANES), F32), pltpu.VMEM((tq, LANES), F32), pltpu.VMEM((tq, HEAD_W), F32),
                        pltpu.VMEM((tq, LANES), F32), pltpu.VMEM((tq, LANES), F32), pltpu.VMEM((tq, HEAD_W), F32)],
        compiler_params=_cparams(("parallel", "parallel", "arbitrary")),
        name="diff_attn",
    )(lam_tab, q_gain, k_gain, sub_gain, bias_tiles, proj, proj, proj)


def _sb_attn_body(q_ref, k_ref, v_ref, o_ref, *scratch, sb, n_sub):
    qi = pl.program_id(2)
    row = lax.broadcasted_iota(I32, (sb, sb), 0)
    col = lax.broadcasted_iota(I32, (sb, sb), 1)
    tri = jnp.where(row >= col, 1.0, 0.0).astype(BF16)
    tri2 = jnp.concatenate([tri, tri], axis=0)
    strict = col < row
    n_rep = sb // LANES

    c_refs, acc_refs = scratch[:n_sub], scratch[n_sub:]
    for ref in scratch:
        ref[...] = jnp.zeros(ref.shape, F32)

    def sweep(it, masked):
        subs = range(n_sub)
        blks = [qi * n_sub + a - it for a in subs]
        rows = [pl.ds(pl.multiple_of(jnp.maximum(b, 0) * sb, sb), sb) for b in blks]
        zs = [lax.dot_general(q_ref[a * sb:(a + 1) * sb, :], k_ref[rows[a], :], _NT,
                              preferred_element_type=F32) * (SB_HEAD_DIM ** -0.5 * LOG2_E) for a in subs]
        log_fails, cums = [], []
        for a in subs:
            log_fail = -(jnp.maximum(zs[a], 0.0) + jnp.log2(1.0 + jnp.exp2(-jnp.abs(zs[a]))))
            if masked:
                log_fail = jnp.where(strict, log_fail, 0.0)
            hi = log_fail.astype(BF16)
            lo = (log_fail - hi.astype(F32)).astype(BF16)
            cums.append(jnp.dot(jnp.concatenate([hi, lo], axis=1), tri2, preferred_element_type=F32))
            log_fails.append(log_fail)
        c_max = jnp.float32(-jnp.inf)
        for a in subs:
            c = c_refs[a][...]
            dead = jnp.where(blks[a] >= 0, 0.0, NEG_BIG)
            w = jnp.exp2(zs[a] + cums[a] + _rep_lanes(c + dead, n_rep))
            if masked:
                w = jnp.where(strict, w, 0.0)
            acc_refs[a][...] += jnp.dot(w.astype(BF16), v_ref[rows[a], :], preferred_element_type=F32)
            c_new = c + jnp.sum(log_fails[a], axis=-1, keepdims=True)
            c_refs[a][...] = c_new
            c_max = jnp.maximum(c_max, jnp.where(blks[a] >= 1, jnp.max(c_new), -jnp.inf))
        return c_max

    c_max = sweep(0, True)

    def cond(carry):
        return carry[1] > SB_SKIP_LOG

    def body(carry):
        return carry[0] + 1, sweep(carry[0], False)

    lax.while_loop(cond, body, (jnp.int32(1), c_max))
    for a in range(n_sub):
        o_ref[a * sb:(a + 1) * sb, :] = acc_refs[a][...].astype(BF16)


def _sb_attention(proj, batch, seq, sb):
    T = proj.shape[0]
    n_sub = _tile(seq // sb, 4)
    tq = sb * n_sub
    nq = seq // tq
    H = SB_HEADS
    base = 3 * DIFF_HEADS
    return pl.pallas_call(
        functools.partial(_sb_attn_body, sb=sb, n_sub=n_sub),
        out_shape=jax.ShapeDtypeStruct((T, BRANCH_WIDTH), BF16),
        grid=(batch, H, nq),
        in_specs=[pl.BlockSpec((tq, HEAD_W), lambda b, h, i: (b * nq + i, base + h)),
                  pl.BlockSpec((seq, HEAD_W), lambda b, h, i: (b, base + H + h)),
                  pl.BlockSpec((seq, HEAD_W), lambda b, h, i: (b, base + 2 * H + h))],
        out_specs=pl.BlockSpec((tq, HEAD_W), lambda b, h, i: (b * nq + i, h)),
        scratch_shapes=([pltpu.VMEM((sb, LANES), F32)] * n_sub + [pltpu.VMEM((sb, HEAD_W), F32)] * n_sub),
        compiler_params=_cparams(("parallel", "parallel", "arbitrary")),
        name="sb_attn",
    )(proj, proj, proj)


def _mla_prep_body(pm_ref, qa_ref, kva_ref, wuq_ref, wukv_ref, qg_ref, kg_ref,
                   cos_ref, sina_ref, sinb_ref, q_ref, k_ref, v_ref):
    x = pm_ref[...].astype(F32)

    def rms(t, g):
        return t * lax.rsqrt(jnp.mean(t * t, axis=-1, keepdims=True) + EPS) * g

    cq = rms(x[:, :MLA_Q_RANK], qa_ref[...]).astype(BF16)
    ckv = rms(x[:, MLA_Q_RANK:MLA_Q_RANK + MLA_KV_RANK], kva_ref[...]).astype(BF16)
    kr = x[:, MLA_Q_RANK + MLA_KV_RANK:MLA_Q_RANK + MLA_KV_RANK + LANES]
    q = jnp.dot(cq, wuq_ref[...], preferred_element_type=F32)
    kv = jnp.dot(ckv, wukv_ref[...], preferred_element_type=F32)
    cos, sina, sinb = cos_ref[...], sina_ref[...], sinb_ref[...]
    half = MLA_ROPE_DIM // 2

    def rope(pe):
        return pe * cos + pltpu.roll(pe, LANES - half, 1) * sina + pltpu.roll(pe, half, 1) * sinb

    k_pe = rope(kr)
    k_pe_ss = jnp.sum(k_pe * k_pe, axis=-1, keepdims=True)
    qk_dim = MLA_NOPE_DIM + MLA_ROPE_DIM
    qg, kg = qg_ref[...], kg_ref[...]
    for h in range(MLA_HEADS):
        o = h * MLA_QK_PAD
        q_n = q[:, o:o + MLA_NOPE_DIM]
        q_pe = rope(q[:, o + MLA_NOPE_DIM:o + MLA_QK_PAD])
        ss = jnp.sum(q_n * q_n, axis=-1, keepdims=True) + jnp.sum(q_pe * q_pe, axis=-1, keepdims=True)
        r = lax.rsqrt(ss * (1.0 / qk_dim) + EPS) * (qk_dim ** -0.5 * LOG2_E)
        q_ref[:, o:o + MLA_NOPE_DIM] = (q_n * r * qg[:, :MLA_NOPE_DIM]).astype(BF16)
        q_ref[:, o + MLA_NOPE_DIM:o + MLA_QK_PAD] = (q_pe * r * qg[:, MLA_NOPE_DIM:]).astype(BF16)
        k_n = kv[:, o:o + MLA_NOPE_DIM]
        ssk = jnp.sum(k_n * k_n, axis=-1, keepdims=True) + k_pe_ss
        rk = lax.rsqrt(ssk * (1.0 / qk_dim) + EPS)
        k_ref[:, o:o + MLA_NOPE_DIM] = (k_n * rk * kg[:, :MLA_NOPE_DIM]).astype(BF16)
        k_ref[:, o + MLA_NOPE_DIM:o + MLA_QK_PAD] = (k_pe * rk * kg[:, MLA_NOPE_DIM:]).astype(BF16)
        v_ref[:, h * MLA_V_DIM:(h + 1) * MLA_V_DIM] = kv[:, o + MLA_NOPE_DIM:o + MLA_QK_PAD].astype(BF16)


def _mla_prep(proj, mla_blk, qa_g, kva_g, w_uq, w_ukv, q_g, k_g, cos_t, sina_t, sinb_t, seq):
    T = proj.shape[0]
    tm = _tile(seq, 512)
    ns = seq // tm
    W = MLA_HEADS * MLA_QK_PAD
    const = lambda i: (0, 0)
    pos = lambda i: (i % ns, 0)
    return pl.pallas_call(
        _mla_prep_body,
        out_shape=(jax.ShapeDtypeStruct((T, W), BF16), jax.ShapeDtypeStruct((T, W), BF16),
                   jax.ShapeDtypeStruct((T, MLA_HEADS * MLA_V_DIM), BF16)),
        grid=(T // tm,),
        in_specs=[pl.BlockSpec((tm, MLA_IN_W), lambda i: (i, mla_blk)),
                  pl.BlockSpec((1, MLA_Q_RANK), const),
                  pl.BlockSpec((1, MLA_KV_RANK), const),
                  pl.BlockSpec((MLA_Q_RANK, W), const),
                  pl.BlockSpec((MLA_KV_RANK, W), const),
                  pl.BlockSpec((1, MLA_QK_PAD), const),
                  pl.BlockSpec((1, MLA_QK_PAD), const),
                  pl.BlockSpec((tm, LANES), pos),
                  pl.BlockSpec((tm, LANES), pos),
                  pl.BlockSpec((tm, LANES), pos)],
        out_specs=(pl.BlockSpec((tm, W), lambda i: (i, 0)),
                   pl.BlockSpec((tm, W), lambda i: (i, 0)),
                   pl.BlockSpec((tm, MLA_HEADS * MLA_V_DIM), lambda i: (i, 0))),
        compiler_params=_cparams(("parallel",)),
        name="mla_prep",
    )(proj, qa_g, kva_g, w_uq, w_ukv, q_g, k_g, cos_t, sina_t, sinb_t)


def _mla_attn_body(q_ref, k_ref, v_ref, o_ref, m, l, acc, *, tq):
    qi = pl.program_id(2)
    q = q_ref[...]
    _softmax_init(m, l, acc)

    def block_rows(j):
        return pl.ds(pl.multiple_of(j * tq, tq), tq)

    def scores(j):
        return lax.dot_general(q, k_ref[block_rows(j), :], _NT, preferred_element_type=F32)

    def update(j, s):
        _softmax_step(s, m, l, acc, v_ref[block_rows(j), :])

    _sweep_far_blocks(qi, scores, update)
    row = lax.broadcasted_iota(I32, (tq, tq), 0)
    col = lax.broadcasted_iota(I32, (tq, tq), 1)
    update(qi, jnp.where(col <= row, scores(qi), NEG_BIG))
    o_ref[...] = _softmax_result(l, acc).astype(BF16)


def _mla_attention(q, k, v, batch, seq, tq):
    T = q.shape[0]
    nq = seq // tq
    return pl.pallas_call(
        functools.partial(_mla_attn_body, tq=tq),
        out_shape=jax.ShapeDtypeStruct((T, BRANCH_WIDTH), BF16),
        grid=(batch, MLA_HEADS, nq),
        in_specs=[pl.BlockSpec((tq, MLA_QK_PAD), lambda b, h, i: (b * nq + i, h)),
                  pl.BlockSpec((seq, MLA_QK_PAD), lambda b, h, i: (b, h)),
                  pl.BlockSpec((seq, MLA_V_DIM), lambda b, h, i: (b, h))],
        out_specs=pl.BlockSpec((tq, MLA_V_DIM), lambda b, h, i: (b * nq + i, h)),
        scratch_shapes=[pltpu.VMEM((tq, LANES), F32), pltpu.VMEM((tq, LANES), F32),
                        pltpu.VMEM((tq, MLA_V_DIM), F32)],
        compiler_params=_cparams(("parallel", "parallel", "arbitrary")),
        name="mla_attn",
    )(q, k, v)


def _merge_body(oa_ref, ob_ref, oc_ref, wb_ref, ga_ref, gb_ref, gc_ref, bg_ref, o_ref):
    acc = None
    for b, (o_r, g_r) in enumerate(((oa_ref, ga_ref), (ob_ref, gb_ref), (oc_ref, gc_ref))):
        gate = jax.nn.sigmoid(g_r[...].astype(F32) + bg_ref[b:b + 1, :])
        t = gate * jnp.dot(o_r[...], wb_ref[0, b], preferred_element_type=F32)
        acc = t if acc is None else acc + t
    o_ref[...] = acc.astype(BF16)


def _branch_merge(o_a, o_b, o_c, w_branch_all, layer, proj, b_gate, d_model):
    T = o_a.shape[0]
    tm, tn = _tile(T, 1024), _tile(d_model, 512)
    assert QKV_W % tn == 0
    g0 = QKV_W // tn
    nj = d_model // tn
    o_spec = pl.BlockSpec((tm, BRANCH_WIDTH), lambda i, j: (i, 0))
    gate_spec = lambda b: pl.BlockSpec((tm, tn), lambda i, j: (i, g0 + b * nj + j))
    return pl.pallas_call(
        _merge_body,
        out_shape=jax.ShapeDtypeStruct((T, d_model), BF16),
        grid=(T // tm, nj),
        in_specs=[o_spec, o_spec, o_spec,
                  pl.BlockSpec((1, N_BRANCH, BRANCH_WIDTH, tn), lambda i, j: (layer, 0, 0, j)),
                  gate_spec(0), gate_spec(1), gate_spec(2),
                  pl.BlockSpec((N_BRANCH, tn), lambda i, j: (0, j))],
        out_specs=pl.BlockSpec((tm, tn), lambda i, j: (i, j)),
        compiler_params=_cparams(("parallel", "arbitrary")),
        name="branch_merge",
    )(o_a, o_b, o_c, w_branch_all, proj, proj, proj, b_gate)


def _out_proj_body(a_ref, w_ref, r_ref, o_ref):
    o_ref[...] = r_ref[...] + jnp.dot(a_ref[...], w_ref[0], preferred_element_type=F32)


def _out_proj(merged, w_out_all, layer, x):
    T, K = merged.shape
    N = w_out_all.shape[2]
    tm, tn = _tile(T, 1024), _tile(N, 1024)
    return pl.pallas_call(
        _out_proj_body,
        out_shape=jax.ShapeDtypeStruct((T, N), F32),
        grid=(T // tm, N // tn),
        in_specs=[pl.BlockSpec((tm, K), lambda i, j: (i, 0)),
                  pl.BlockSpec((1, K, tn), lambda i, j: (layer, 0, j)),
                  pl.BlockSpec((tm, tn), lambda i, j: (i, j))],
        out_specs=pl.BlockSpec((tm, tn), lambda i, j: (i, j)),
        compiler_params=_cparams(("parallel", "arbitrary")),
        name="out_proj",
    )(merged, w_out_all, x)


def _route_body(x_ref, g_ref, wr_ref, br_ref, comb_ref, meta_ref, cnt_ref, carry_s, *, tm):
    i = pl.program_id(0)

    @pl.when(i == 0)
    def _():
        carry_s[...] = jnp.zeros(carry_s.shape, F32)

    x = x_ref[...]
    h = x * lax.rsqrt(jnp.mean(x * x, axis=-1, keepdims=True) + EPS) * g_ref[...]
    logits = jnp.dot(h, wr_ref[...], preferred_element_type=F32,
                     precision=lax.Precision.HIGHEST) + br_ref[...]
    lane = lax.broadcasted_iota(I32, (tm, LANES), 1)
    big = jnp.int32(LANES)

    gl = jnp.where(lane < N_GROUPS, logits, -jnp.inf)
    g_max = jnp.max(gl, axis=-1, keepdims=True)
    g_idx = jnp.min(jnp.where(gl == g_max, lane, big), axis=-1, keepdims=True)
    g_val = 1.0 / jnp.sum(jnp.exp(gl - g_max), axis=-1, keepdims=True)

    e_lo = N_GROUPS + g_idx * EXPERTS_PER_GROUP
    el = jnp.where(jnp.logical_and(lane >= e_lo, lane < e_lo + EXPERTS_PER_GROUP), logits, -jnp.inf)
    e1 = jnp.max(el, axis=-1, keepdims=True)
    i1 = jnp.min(jnp.where(el == e1, lane, big), axis=-1, keepdims=True)
    el2 = jnp.where(lane == i1, -jnp.inf, el)
    e2 = jnp.max(el2, axis=-1, keepdims=True)
    i2 = jnp.min(jnp.where(el2 == e2, lane, big), axis=-1, keepdims=True)
    t = jnp.exp(e2 - e1)
    w1 = g_val / (1.0 + t)
    w2 = g_val * t / (1.0 + t)
    comb_ref[...] = jnp.where(lane == i1, w1, 0.0) + jnp.where(lane == i2, w2, 0.0)

    onehot = jnp.where(lane == g_idx, 1.0, 0.0)
    r = lax.broadcasted_iota(I32, (tm, tm), 0)
    c = lax.broadcasted_iota(I32, (tm, tm), 1)
    before = jnp.where(c < r, 1.0, 0.0).astype(BF16)
    prefix = jnp.dot(before, onehot.astype(BF16), preferred_element_type=F32) + carry_s[...]
    rank = jnp.sum(jnp.where(lane == g_idx, prefix, 0.0), axis=-1, keepdims=True)
    carry_s[...] += jnp.sum(onehot, axis=0, keepdims=True)
    meta_ref[...] = jnp.where(lane == 0, g_idx, jnp.where(lane == 1, rank.astype(I32), 0))
    cnt_ref[...] = jnp.broadcast_to(carry_s[...], cnt_ref.shape).astype(I32)


def _moe_route(x, g, w_r, b_r):
    T, D = x.shape
    tm = _tile(T, 512)
    return pl.pallas_call(
        functools.partial(_route_body, tm=tm),
        out_shape=(jax.ShapeDtypeStruct((T, LANES), F32),
                   jax.ShapeDtypeStruct((T, LANES), I32), jax.ShapeDtypeStruct((8, LANES), I32)),
        grid=(T // tm,),
        in_specs=[pl.BlockSpec((tm, D), lambda i: (i, 0)),
                  pl.BlockSpec((1, D), lambda i: (0, 0)),
                  pl.BlockSpec((D, LANES), lambda i: (0, 0)),
                  pl.BlockSpec((1, LANES), lambda i: (0, 0))],
        out_specs=(pl.BlockSpec((tm, LANES), lambda i: (i, 0)),
                   pl.BlockSpec((tm, LANES), lambda i: (i, 0)),
                   pl.BlockSpec((8, LANES), lambda i: (0, 0))),
        scratch_shapes=[pltpu.VMEM((1, LANES), F32)],
        compiler_params=_cparams(("arbitrary",)),
        name="moe_route",
    )(x, g.reshape(1, D), w_r, b_r)


def _dispatch_body(dest_ref, x_ref, g_ref, comb_ref, xs_in, xs_ref, row_buf, sem, *, tm):
    del xs_in
    d_model = x_ref.shape[1]
    x = x_ref[...]
    row_buf[:, :d_model] = x * lax.rsqrt(jnp.mean(x * x, axis=-1, keepdims=True) + EPS) * g_ref[...]
    row_buf[:, d_model:] = comb_ref[...]
    base = pl.program_id(0) * tm

    def row_copy(r):
        return pltpu.make_async_copy(row_buf.at[pl.ds(r, 1)], xs_ref.at[pl.ds(dest_ref[base + r], 1)], sem.at[0])

    def issue(r, carry):
        row_copy(r).start()
        return carry
    lax.fori_loop(0, tm, issue, 0)

    def drain(r, carry):
        row_copy(r).wait()
        return carry
    lax.fori_loop(0, tm, drain, 0)


def _moe_dispatch(dest, x, g, comb, t_pad):
    T, D = x.shape
    W = D + LANES
    tm = _tile(T, 256)
    return pl.pallas_call(
        functools.partial(_dispatch_body, tm=tm),
        out_shape=jax.ShapeDtypeStruct((t_pad, W), F32),
        grid_spec=pltpu.PrefetchScalarGridSpec(
            num_scalar_prefetch=1, grid=(T // tm,),
            in_specs=[pl.BlockSpec((tm, D), lambda i, d: (i, 0)),
                      pl.BlockSpec((1, D), lambda i, d: (0, 0)),
                      pl.BlockSpec((tm, LANES), lambda i, d: (i, 0)),
                      pl.BlockSpec(memory_space=pl.ANY)],
            out_specs=pl.BlockSpec(memory_space=pl.ANY),
            scratch_shapes=[pltpu.VMEM((tm, W), F32), pltpu.SemaphoreType.DMA((1,))]),
        input_output_aliases={4: 0},
        compiler_params=_cparams(("arbitrary",)),
        name="moe_dispatch",
    )(dest, x, g.reshape(1, D), comb, jnp.zeros((t_pad, W), F32))


def _gather_rows_into(idx_ref, table_ref, dst_ref, sem, n_rows):
    base = pl.program_id(0) * n_rows

    def row_copy(r):
        return pltpu.make_async_copy(table_ref.at[pl.ds(idx_ref[base + r], 1)], dst_ref.at[pl.ds(r, 1)], sem.at[0])

    def issue(r, carry):
        row_copy(r).start()
        return carry
    lax.fori_loop(0, n_rows, issue, 0)

    def drain(r, carry):
        row_copy(r).wait()
        return carry
    lax.fori_loop(0, n_rows, drain, 0)


def _experts_body(tg_ref, nt_ref, xs_ref, wgu_ref, wd_ref, y_ref, *, d_expert):
    i, e = pl.program_id(0), pl.program_id(1)
    valid = i < nt_ref[0]
    d_model = y_ref.shape[1]

    @pl.when(valid)
    def _():
        x = xs_ref[:, :d_model].astype(BF16)
        gu = jnp.dot(x, wgu_ref[0], preferred_element_type=F32)
        comb = xs_ref[:, d_model:]
        lane = lax.broadcasted_iota(I32, comb.shape, 1)
        sel = N_GROUPS + tg_ref[i] * EXPERTS_PER_GROUP + e
        c = jnp.sum(jnp.where(lane == sel, comb, 0.0), axis=-1, keepdims=True)
        a = (jax.nn.silu(gu[:, :d_expert]) * gu[:, d_expert:] * c).astype(BF16)
        y = jnp.dot(a, wd_ref[0], preferred_element_type=F32)

        @pl.when(e == 0)
        def _():
            y_ref[...] = y

        @pl.when(e > 0)
        def _():
            y_ref[...] += y

    @pl.when(jnp.logical_and(jnp.logical_not(valid), e == 0))
    def _():
        y_ref[...] = jnp.zeros(y_ref.shape, F32)


def _moe_experts(tile_group, n_tiles, xs, w_gu, w_down, layer, tg):
    t_pad = xs.shape[0]
    d_expert, D = w_down.shape[1:]
    expert = lambda i, e, tgr, ntr: (layer * N_EXPERTS + tgr[i] * EXPERTS_PER_GROUP + e, 0, 0)
    rows = lambda i, e, tgr, ntr: (i, 0)
    return pl.pallas_call(
        functools.partial(_experts_body, d_expert=d_expert),
        out_shape=jax.ShapeDtypeStruct((t_pad, D), F32),
        grid_spec=pltpu.PrefetchScalarGridSpec(
            num_scalar_prefetch=2, grid=(t_pad // tg, EXPERTS_PER_GROUP),
            in_specs=[pl.BlockSpec((tg, D + LANES), rows),
                      pl.BlockSpec((1, D, 2 * d_expert), expert),
                      pl.BlockSpec((1, d_expert, D), expert)],
            out_specs=pl.BlockSpec((tg, D), rows)),
        compiler_params=_cparams(("parallel", "arbitrary")),
        name="moe_experts",
    )(tile_group, n_tiles, xs, w_gu, w_down)


def _combine_body(dest_ref, x_ref, ys_ref, o_ref, buf, sem, *, tm):
    _gather_rows_into(dest_ref, ys_ref, buf, sem, tm)
    o_ref[...] = x_ref[...] + buf[...]


def _moe_combine(dest, x, ys):
    T, D = x.shape
    tm = _tile(T, 256)
    return pl.pallas_call(
        functools.partial(_combine_body, tm=tm),
        out_shape=jax.ShapeDtypeStruct((T, D), F32),
        grid_spec=pltpu.PrefetchScalarGridSpec(
            num_scalar_prefetch=1, grid=(T // tm,),
            in_specs=[pl.BlockSpec((tm, D), lambda i, d: (i, 0)),
                      pl.BlockSpec(memory_space=pl.ANY)],
            out_specs=pl.BlockSpec((tm, D), lambda i, d: (i, 0)),
            scratch_shapes=[pltpu.VMEM((tm, D), F32), pltpu.SemaphoreType.DMA((1,))]),
        compiler_params=_cparams(("arbitrary",)),
        name="moe_combine",
    )(dest, x, ys)


def _hier_moe(x, ln_g, w_r, b_r, w_gu, w_down, layer):
    T, D = x.shape
    tg = _tile(T, 512)
    comb, meta, counts = _moe_route(x, ln_g, w_r, b_r)
    cnt = counts[0, :N_GROUPS]
    seg_tiles = (cnt + tg - 1) // tg
    tile_end = jnp.cumsum(seg_tiles)
    seg_off = (tile_end - seg_tiles) * tg
    dest = seg_off[meta[:, 0]] + meta[:, 1]
    n_tiles_max = T // tg + N_GROUPS
    n_tiles = tile_end[-1]
    tile_ids = jnp.arange(n_tiles_max, dtype=I32)
    tile_group = jnp.searchsorted(tile_end, jnp.minimum(tile_ids, n_tiles - 1), side="right").astype(I32)
    dest = dest.astype(I32)
    xs = _moe_dispatch(dest, x, ln_g, comb, n_tiles_max * tg)
    ys = _moe_experts(tile_group, n_tiles.reshape(1).astype(I32), xs, w_gu, w_down, layer, tg)
    return _moe_combine(dest, x, ys)


def _layout_w_in(w_in, d_model):
    depth = w_in.shape[0]
    mla_w = MLA_Q_RANK + MLA_KV_RANK + MLA_ROPE_DIM
    gate_w = N_BRANCH * d_model
    mla_off = _round_up(QKV_W + gate_w, MLA_IN_W)
    n_pad = _round_up(mla_off + MLA_IN_W, 1024)
    w = jnp.concatenate([
        w_in[:, :, :QKV_W].astype(BF16),
        w_in[:, :, QKV_W + mla_w:].astype(BF16),
        jnp.zeros((depth, d_model, mla_off - QKV_W - gate_w), BF16),
        w_in[:, :, QKV_W:QKV_W + mla_w].astype(BF16),
        jnp.zeros((depth, d_model, n_pad - mla_off - mla_w), BF16)], axis=2)
    return w, mla_off // MLA_IN_W


def _layout_w_uq(w_uq):
    w = w_uq.reshape(MLA_Q_RANK, MLA_HEADS, MLA_NOPE_DIM + MLA_ROPE_DIM)
    w = jnp.pad(w, ((0, 0), (0, 0), (0, MLA_QK_PAD - MLA_NOPE_DIM - MLA_ROPE_DIM)))
    return w.reshape(MLA_Q_RANK, MLA_HEADS * MLA_QK_PAD).astype(BF16)


def _pad_gain(g):
    return jnp.pad(g, (0, MLA_QK_PAD - g.shape[0])).reshape(1, MLA_QK_PAD)


def _rope_tables(seq):
    pos = jnp.arange(seq, dtype=F32)
    inv_freq = 1.0 / (ROPE_THETA ** (jnp.arange(0, MLA_ROPE_DIM, 2, dtype=F32) / MLA_ROPE_DIM))
    ang = pos[:, None] * inv_freq[None, :]
    cos, sin = jnp.cos(ang), jnp.sin(ang)
    z = jnp.zeros_like(cos)
    return (jnp.concatenate([cos, cos, z, z], axis=1),
            jnp.concatenate([-sin, z, z, z], axis=1),
            jnp.concatenate([z, sin, z, z], axis=1))


def kernel(x, rel_bias, ln1_g, w_in, diff_q_norm, diff_k_norm, diff_lambda, diff_subln_g, mla_q_a_norm, mla_w_uq, mla_kv_a_norm, mla_w_ukv, mla_q_norm, mla_k_norm, w_branch, b_gate, w_out, ln2_g, router_group_w, router_group_b, router_expert_w, router_expert_b, expert_w_gu, expert_w_down):
    B, S, D = x.shape
    depth = w_in.shape[0]
    T = B * S
    tq = _tile(S, 512)
    tq_sb = _tile(S, 256)
    cos_t, sina_t, sinb_t = _rope_tables(S)
    bias_tiles = _t5_bias_tiles(rel_bias, tq)
    w_in_all, mla_blk = _layout_w_in(w_in, D)
    w_branch_all = w_branch.astype(BF16)
    w_out_all = w_out.astype(BF16)
    w_gu_all = expert_w_gu.astype(BF16).reshape((depth * N_EXPERTS,) + expert_w_gu.shape[2:])
    w_down_all = expert_w_down.astype(BF16).reshape((depth * N_EXPERTS,) + expert_w_down.shape[2:])
    xt = x.reshape(T, D)
    for l in range(depth):
        proj = _rms_matmul(xt, ln1_g[l], w_in_all, l, BF16, "rms_in_proj")

        lam_init = 0.8 - 0.6 * math.exp(-0.3 * l)
        lam_tab = jnp.concatenate([diff_lambda[l], jnp.full((4, DIFF_QK_DIM), lam_init, F32)], axis=0)
        o_a = _diff_attention(proj, lam_tab,
                              jnp.tile(diff_q_norm[l], 2).reshape(1, HEAD_W),
                              jnp.tile(diff_k_norm[l], 2).reshape(1, HEAD_W),
                              diff_subln_g[l].reshape(1, HEAD_W), bias_tiles, B, S, tq)
        o_b = _sb_attention(proj, B, S, tq_sb)
        q_c, k_c, v_c = _mla_prep(proj, mla_blk, mla_q_a_norm[l].reshape(1, -1), mla_kv_a_norm[l].reshape(1, -1),
                                  _layout_w_uq(mla_w_uq[l]), mla_w_ukv[l].astype(BF16),
                                  _pad_gain(mla_q_norm[l]), _pad_gain(mla_k_norm[l]),
                                  cos_t, sina_t, sinb_t, S)
        o_c = _mla_attention(q_c, k_c, v_c, B, S, tq)

        merged = _branch_merge(o_a, o_b, o_c, w_branch_all, l, proj, b_gate[l].reshape(N_BRANCH, D), D)
        xt = _out_proj(merged, w_out_all, l, xt)

        w_r = jnp.concatenate([router_group_w[l], router_expert_w[l],
                               jnp.zeros((D, LANES - N_GROUPS - N_EXPERTS), F32)], axis=1)
        b_r = jnp.concatenate([router_group_b[l], router_expert_b[l],
                               jnp.zeros((LANES - N_GROUPS - N_EXPERTS,), F32)]).reshape(1, LANES)
        xt = _hier_moe(xt, ln2_g[l], w_r, b_r, w_gu_all, w_down_all, l)
    return xt.reshape(B, S, D)
```

```python
import functools
import math

import jax
import jax.numpy as jnp
from jax import lax
from jax.experimental import pallas as pl
from jax.experimental.pallas import tpu as pltpu

F32 = jnp.float32
BF16 = jnp.bfloat16
I32 = jnp.int32

EPS = 1e-6
LANES = 128
HEAD_W = 128
DIFF_HEADS = 8
DIFF_QK_DIM = 64
SB_HEADS = 8
SB_HEAD_DIM = 128
MLA_HEADS = 8
MLA_Q_RANK = 512
MLA_KV_RANK = 256
MLA_NOPE_DIM = 128
MLA_ROPE_DIM = 64
MLA_V_DIM = 128
MLA_QK_PAD = 256
ROPE_THETA = 10000.0
REL_BUCKETS = 32
REL_MAX_DIST = 128
N_GROUPS = 8
EXPERTS_PER_GROUP = 4
N_EXPERTS = N_GROUPS * EXPERTS_PER_GROUP
N_BRANCH = 3
BRANCH_WIDTH = 1024
QKV_W = 6 * BRANCH_WIDTH
MLA_IN_W = 1024
NEG_BIG = -1e30
LOG2_E = math.log2(math.e)
SB_SKIP_LOG = -100.0 * LOG2_E
VMEM_LIMIT = 56 * 1024 * 1024
ROW_DMA_TILE = 512
ROW_DMA_UNROLL = 8

_NT = (((1,), (1,)), ((), ()))


def _cparams(sem, vmem=VMEM_LIMIT):
    return pltpu.CompilerParams(dimension_semantics=sem, vmem_limit_bytes=vmem)


def _tile(n, target):
    t = min(n, target)
    while n % t:
        t -= 1
    return t


def _round_up(n, m):
    return -(-n // m) * m


def _rms_mm_body(x_ref, g_ref, wa_ref, wb_ref, o_ref, h_ref, *, n_a):
    j = pl.program_id(1)

    @pl.when(j == 0)
    def _():
        x = x_ref[...]
        ms = jnp.mean(x * x, axis=-1, keepdims=True)
        h_ref[...] = (x * lax.rsqrt(ms + EPS) * g_ref[...]).astype(BF16)

    @pl.when(j < n_a)
    def _():
        o_ref[...] = jnp.dot(h_ref[...], wa_ref[0], preferred_element_type=F32).astype(o_ref.dtype)

    @pl.when(j >= n_a)
    def _():
        o_ref[...] = jnp.dot(h_ref[...], wb_ref[0], preferred_element_type=F32).astype(o_ref.dtype)


def _rms_matmul(x, g, wa_all, wb_all, layer, out_dtype, name):
    T, K = x.shape
    na_cols, nb_cols = wa_all.shape[2], wb_all.shape[2]
    tm, tn = _tile(T, 1024), _tile(math.gcd(na_cols, nb_cols), 1024)
    n_a, n_b = na_cols // tn, nb_cols // tn
    return pl.pallas_call(
        functools.partial(_rms_mm_body, n_a=n_a),
        out_shape=jax.ShapeDtypeStruct((T, na_cols + nb_cols), out_dtype),
        grid=(T // tm, n_a + n_b),
        in_specs=[pl.BlockSpec((tm, K), lambda i, j: (i, 0)),
                  pl.BlockSpec((1, K), lambda i, j: (0, 0)),
                  pl.BlockSpec((1, K, tn), lambda i, j: (layer, 0, jnp.minimum(j, n_a - 1))),
                  pl.BlockSpec((1, K, tn), lambda i, j: (layer, 0, jnp.maximum(j - n_a, 0)))],
        out_specs=pl.BlockSpec((tm, tn), lambda i, j: (i, j)),
        scratch_shapes=[pltpu.VMEM((tm, K), BF16)],
        compiler_params=_cparams(("parallel", "arbitrary")),
        name=name,
    )(x, g.reshape(1, K), wa_all, wb_all)


def _rep_lanes(x, n):
    return jnp.concatenate([x] * n, axis=1)


def _softmax_step(s, m_ref, l_ref, acc_ref, v):
    n_rep = s.shape[1] // LANES
    m_prev = m_ref[...]
    m_new = jnp.maximum(m_prev, jnp.max(s, axis=-1, keepdims=True))
    alpha = jnp.exp2(m_prev - m_new)
    p = jnp.exp2(s - _rep_lanes(m_new, n_rep))
    p_sum = p[:, :LANES]
    for c in range(1, n_rep):
        p_sum = p_sum + p[:, c * LANES:(c + 1) * LANES]
    l_ref[...] = alpha * l_ref[...] + p_sum
    acc_ref[...] = alpha * acc_ref[...] + jnp.dot(p.astype(BF16), v, preferred_element_type=F32)
    m_ref[...] = m_new


def _softmax_result(l_ref, acc_ref):
    return acc_ref[...] / jnp.sum(l_ref[...], axis=-1, keepdims=True)


def _sweep_far_blocks(n_far, scores, update):
    def pair(p, carry):
        s_a = scores(2 * p)
        s_b = scores(2 * p + 1)
        update(2 * p, s_a)
        update(2 * p + 1, s_b)
        return carry
    lax.fori_loop(0, n_far // 2, pair, 0)

    @pl.when(n_far % 2 == 1)
    def _():
        update(n_far - 1, scores(n_far - 1))


def _softmax_init(m_ref, l_ref, acc_ref):
    m_ref[...] = jnp.full(m_ref.shape, NEG_BIG, F32)
    l_ref[...] = jnp.zeros(l_ref.shape, F32)
    acc_ref[...] = jnp.zeros(acc_ref.shape, F32)


def _t5_tiles_body(rb_ref, o_ref, *, tq):
    h = pl.program_id(0)
    row = lax.broadcasted_iota(I32, (tq, tq), 0)
    col = lax.broadcasted_iota(I32, (tq, tq), 1)
    max_exact = REL_BUCKETS // 2
    far = rb_ref[(REL_BUCKETS - 1) * DIFF_HEADS + h]
    for which in range(2):
        d = row - col + which * tq
        n = jnp.maximum(d, 0)
        nf = jnp.maximum(n, 1).astype(F32)
        large = max_exact + (jnp.log(nf / max_exact) / math.log(REL_MAX_DIST / max_exact)
                             * (REL_BUCKETS - max_exact)).astype(I32)
        large = jnp.minimum(large, REL_BUCKETS - 1)
        bucket = jnp.where(n < max_exact, n, large)
        val = jnp.zeros((tq, tq), F32)
        for b in range(REL_BUCKETS - 1):
            val = jnp.where(bucket == b, (rb_ref[b * DIFF_HEADS + h] - far) * LOG2_E, val)
        o_ref[0, which] = jnp.where(d >= 0, val, NEG_BIG)


def _t5_bias_tiles(rel_bias, tq):
    assert tq >= REL_MAX_DIST
    return pl.pallas_call(
        functools.partial(_t5_tiles_body, tq=tq),
        out_shape=jax.ShapeDtypeStruct((DIFF_HEADS, 2, tq, tq), F32),
        grid=(DIFF_HEADS,),
        in_specs=[pl.BlockSpec(memory_space=pltpu.SMEM)],
        out_specs=pl.BlockSpec((1, 2, tq, tq), lambda h: (h, 0, 0, 0)),
        compiler_params=_cparams(("parallel",)),
        name="t5_bias_tiles",
    )(rel_bias.reshape(-1))


def _diff_attn_body(lam_ref, qg_ref, kg_ref, sg_ref, bias_ref, q_ref, k_ref, v_ref, o_ref,
                    kn_s, m1, l1, a1, m2, l2, a2, *, tq, seq):
    qi = pl.program_id(2)
    lo = lax.broadcasted_iota(I32, (1, HEAD_W), 1) < DIFF_QK_DIM

    def half_rms(x, g):
        x2 = x * x
        s_lo = jnp.sum(jnp.where(lo, x2, 0.0), axis=-1, keepdims=True)
        s_hi = jnp.sum(jnp.where(lo, 0.0, x2), axis=-1, keepdims=True)
        r = lax.rsqrt(jnp.where(lo, s_lo, s_hi) * (1.0 / DIFF_QK_DIM) + EPS)
        return x * r * g

    @pl.when(qi == 0)
    def _():
        def norm_chunk(c, carry):
            rows = pl.ds(pl.multiple_of(c * tq, tq), tq)
            kn_s[rows, :] = half_rms(k_ref[rows, :].astype(F32), kg_ref[...]).astype(BF16)
            return carry
        lax.fori_loop(0, seq // tq, norm_chunk, 0)

    qn = half_rms(q_ref[...].astype(F32), qg_ref[...]) * (DIFF_QK_DIM ** -0.5 * LOG2_E)
    q1 = jnp.where(lo, qn, 0.0).astype(BF16)
    q2 = jnp.where(lo, 0.0, qn).astype(BF16)

    _softmax_init(m1, l1, a1)
    _softmax_init(m2, l2, a2)

    def block_rows(j):
        return pl.ds(pl.multiple_of(j * tq, tq), tq)

    def scores(j):
        kb = kn_s[block_rows(j), :]
        return (lax.dot_general(q1, kb, _NT, preferred_element_type=F32),
                lax.dot_general(q2, kb, _NT, preferred_element_type=F32))

    def update(j, s, bias=None):
        s1, s2 = s if bias is None else (s[0] + bias, s[1] + bias)
        vb = v_ref[block_rows(j), :]
        _softmax_step(s1, m1, l1, a1, vb)
        _softmax_step(s2, m2, l2, a2, vb)

    _sweep_far_blocks(jnp.maximum(qi - 1, 0), scores, update)

    @pl.when(qi >= 1)
    def _():
        update(qi - 1, scores(qi - 1), bias_ref[0, 1])

    update(qi, scores(qi), bias_ref[0, 0])

    lp = lam_ref[...]
    lam_init = lp[4:5, 0:1]
    lam = (jnp.exp(jnp.sum(lp[0:1] * lp[1:2], axis=-1, keepdims=True))
           - jnp.exp(jnp.sum(lp[2:3] * lp[3:4], axis=-1, keepdims=True)) + lam_init)
    o = _softmax_result(l1, a1) - lam * _softmax_result(l2, a2)
    ms = jnp.mean(o * o, axis=-1, keepdims=True)
    o = o * lax.rsqrt(ms + EPS) * sg_ref[...] * (1.0 - lam_init)
    o_ref[...] = o.astype(BF16)


def _diff_attention(proj, lam_tab, q_gain, k_gain, sub_gain, bias_tiles, batch, seq, tq):
    T = proj.shape[0]
    nq = seq // tq
    H = DIFF_HEADS
    small = lambda b, h, i: (0, 0)
    return pl.pallas_call(
        functools.partial(_diff_attn_body, tq=tq, seq=seq),
        out_shape=jax.ShapeDtypeStruct((T, BRANCH_WIDTH), BF16),
        grid=(batch, H, nq),
        in_specs=[pl.BlockSpec((8, DIFF_QK_DIM), small),
                  pl.BlockSpec((1, HEAD_W), small),
                  pl.BlockSpec((1, HEAD_W), small),
                  pl.BlockSpec((1, HEAD_W), small),
                  pl.BlockSpec((1, 2, tq, tq), lambda b, h, i: (h, 0, 0, 0)),
                  pl.BlockSpec((tq, HEAD_W), lambda b, h, i: (b * nq + i, h)),
                  pl.BlockSpec((seq, HEAD_W), lambda b, h, i: (b, H + h)),
                  pl.BlockSpec((seq, HEAD_W), lambda b, h, i: (b, 2 * H + h))],
        out_specs=pl.BlockSpec((tq, HEAD_W), lambda b, h, i: (b * nq + i, h)),
        scratch_shapes=[pltpu.VMEM((seq, HEAD_W), BF16),
                        pltpu.VMEM((tq, LANES), F32), pltpu.VMEM((tq, LANES), F32), pltpu.VMEM((tq, HEAD_W), F32),
                        pltpu.VMEM((tq, LANES), F32), pltpu.VMEM((tq, LANES), F32), pltpu.VMEM((tq, HEAD_W), F32)],
        compiler_params=_cparams(("parallel", "parallel", "arbitrary")),
        name="diff_attn",
    )(lam_tab, q_gain, k_gain, sub_gain, bias_tiles, proj, proj, proj)


def _sb_attn_body(q_ref, k_ref, v_ref, o_ref, *scratch, sb, n_sub):
    qi = pl.program_id(2)
    row = lax.broadcasted_iota(I32, (sb, sb), 0)
    col = lax.broadcasted_iota(I32, (sb, sb), 1)
    tri = jnp.where(row >= col, 1.0, 0.0).astype(BF16)
    tri2 = jnp.concatenate([tri, tri], axis=0)
    strict = col < row
    n_rep = sb // LANES

    c_refs, acc_refs = scratch[:n_sub], scratch[n_sub:]
    for ref in scratch:
        ref[...] = jnp.zeros(ref.shape, F32)

    def sweep(it, masked):
        subs = range(n_sub)
        blks = [qi * n_sub + a - it for a in subs]
        rows = [pl.ds(pl.multiple_of(jnp.maximum(b, 0) * sb, sb), sb) for b in blks]
        zs = [lax.dot_general(q_ref[a * sb:(a + 1) * sb, :], k_ref[rows[a], :], _NT,
                              preferred_element_type=F32) * (SB_HEAD_DIM ** -0.5 * LOG2_E) for a in subs]
        log_fails, cums = [], []
        for a in subs:
            log_fail = -(jnp.maximum(zs[a], 0.0) + jnp.log2(1.0 + jnp.exp2(-jnp.abs(zs[a]))))
            if masked:
                log_fail = jnp.where(strict, log_fail, 0.0)
            hi = log_fail.astype(BF16)
            lo = (log_fail - hi.astype(F32)).astype(BF16)
            cums.append(jnp.dot(jnp.concatenate([hi, lo], axis=1), tri2, preferred_element_type=F32))
            log_fails.append(log_fail)
        c_max = jnp.float32(-jnp.inf)
        for a in subs:
            c = c_refs[a][...]
            dead = jnp.where(blks[a] >= 0, 0.0, NEG_BIG)
            w = jnp.exp2(zs[a] + cums[a] + _rep_lanes(c + dead, n_rep))
            if masked:
                w = jnp.where(strict, w, 0.0)
            acc_refs[a][...] += jnp.dot(w.astype(BF16), v_ref[rows[a], :], preferred_element_type=F32)
            c_new = c + jnp.sum(log_fails[a], axis=-1, keepdims=True)
            c_refs[a][...] = c_new
            c_max = jnp.maximum(c_max, jnp.where(blks[a] >= 1, jnp.max(c_new), -jnp.inf))
        return c_max

    c_max = sweep(0, True)

    def cond(carry):
        return carry[1] > SB_SKIP_LOG

    def body(carry):
        return carry[0] + 1, sweep(carry[0], False)

    lax.while_loop(cond, body, (jnp.int32(1), c_max))
    for a in range(n_sub):
        o_ref[a * sb:(a + 1) * sb, :] = acc_refs[a][...].astype(BF16)


def _sb_attention(proj, batch, seq, sb):
    T = proj.shape[0]
    n_sub = _tile(seq // sb, 4)
    tq = sb * n_sub
    nq = seq // tq
    H = SB_HEADS
    base = 3 * DIFF_HEADS
    return pl.pallas_call(
        functools.partial(_sb_attn_body, sb=sb, n_sub=n_sub),
        out_shape=jax.ShapeDtypeStruct((T, BRANCH_WIDTH), BF16),
        grid=(batch, H, nq),
        in_specs=[pl.BlockSpec((tq, HEAD_W), lambda b, h, i: (b * nq + i, base + h)),
                  pl.BlockSpec((seq, HEAD_W), lambda b, h, i: (b, base + H + h)),
                  pl.BlockSpec((seq, HEAD_W), lambda b, h, i: (b, base + 2 * H + h))],
        out_specs=pl.BlockSpec((tq, HEAD_W), lambda b, h, i: (b * nq + i, h)),
        scratch_shapes=([pltpu.VMEM((sb, LANES), F32)] * n_sub + [pltpu.VMEM((sb, HEAD_W), F32)] * n_sub),
        compiler_params=_cparams(("parallel", "parallel", "arbitrary")),
        name="sb_attn",
    )(proj, proj, proj)


def _mla_prep_body(pm_ref, qa_ref, kva_ref, wuq_ref, wukv_ref, qg_ref, kg_ref,
                   cos_ref, sina_ref, sinb_ref, q_ref, k_ref, v_ref):
    x = pm_ref[...].astype(F32)

    def rms(t, g):
        return t * lax.rsqrt(jnp.mean(t * t, axis=-1, keepdims=True) + EPS) * g

    cq = rms(x[:, :MLA_Q_RANK], qa_ref[...]).astype(BF16)
    ckv = rms(x[:, MLA_Q_RANK:MLA_Q_RANK + MLA_KV_RANK], kva_ref[...]).astype(BF16)
    kr = x[:, MLA_Q_RANK + MLA_KV_RANK:MLA_Q_RANK + MLA_KV_RANK + LANES]
    q = jnp.dot(cq, wuq_ref[...], preferred_element_type=F32)
    kv = jnp.dot(ckv, wukv_ref[...], preferred_element_type=F32)
    cos, sina, sinb = cos_ref[...], sina_ref[...], sinb_ref[...]
    half = MLA_ROPE_DIM // 2

    def rope(pe):
        return pe * cos + pltpu.roll(pe, LANES - half, 1) * sina + pltpu.roll(pe, half, 1) * sinb

    k_pe = rope(kr)
    k_pe_ss = jnp.sum(k_pe * k_pe, axis=-1, keepdims=True)
    qk_dim = MLA_NOPE_DIM + MLA_ROPE_DIM
    qg, kg = qg_ref[...], kg_ref[...]
    for h in range(MLA_HEADS):
        o = h * MLA_QK_PAD
        q_n = q[:, o:o + MLA_NOPE_DIM]
        q_pe = rope(q[:, o + MLA_NOPE_DIM:o + MLA_QK_PAD])
        ss = jnp.sum(q_n * q_n, axis=-1, keepdims=True) + jnp.sum(q_pe * q_pe, axis=-1, keepdims=True)
        r = lax.rsqrt(ss * (1.0 / qk_dim) + EPS) * (qk_dim ** -0.5 * LOG2_E)
        q_ref[:, o:o + MLA_NOPE_DIM] = (q_n * r * qg[:, :MLA_NOPE_DIM]).astype(BF16)
        q_ref[:, o + MLA_NOPE_DIM:o + MLA_QK_PAD] = (q_pe * r * qg[:, MLA_NOPE_DIM:]).astype(BF16)
        k_n = kv[:, o:o + MLA_NOPE_DIM]
        ssk = jnp.sum(k_n * k_n, axis=-1, keepdims=True) + k_pe_ss
        rk = lax.rsqrt(ssk * (1.0 / qk_dim) + EPS)
        k_ref[:, o:o + MLA_NOPE_DIM] = (k_n * rk * kg[:, :MLA_NOPE_DIM]).astype(BF16)
        k_ref[:, o + MLA_NOPE_DIM:o + MLA_QK_PAD] = (k_pe * rk * kg[:, MLA_NOPE_DIM:]).astype(BF16)
        v_ref[:, h * MLA_V_DIM:(h + 1) * MLA_V_DIM] = kv[:, o + MLA_NOPE_DIM:o + MLA_QK_PAD].astype(BF16)


def _mla_prep(proj, mla_blk, qa_g, kva_g, w_uq, w_ukv, q_g, k_g, cos_t, sina_t, sinb_t, seq):
    T = proj.shape[0]
    tm = _tile(seq, 512)
    ns = seq // tm
    W = MLA_HEADS * MLA_QK_PAD
    const = lambda i: (0, 0)
    pos = lambda i: (i % ns, 0)
    return pl.pallas_call(
        _mla_prep_body,
        out_shape=(jax.ShapeDtypeStruct((T, W), BF16), jax.ShapeDtypeStruct((T, W), BF16),
                   jax.ShapeDtypeStruct((T, MLA_HEADS * MLA_V_DIM), BF16)),
        grid=(T // tm,),
        in_specs=[pl.BlockSpec((tm, MLA_IN_W), lambda i: (i, mla_blk)),
                  pl.BlockSpec((1, MLA_Q_RANK), const),
                  pl.BlockSpec((1, MLA_KV_RANK), const),
                  pl.BlockSpec((MLA_Q_RANK, W), const),
                  pl.BlockSpec((MLA_KV_RANK, W), const),
                  pl.BlockSpec((1, MLA_QK_PAD), const),
                  pl.BlockSpec((1, MLA_QK_PAD), const),
                  pl.BlockSpec((tm, LANES), pos),
                  pl.BlockSpec((tm, LANES), pos),
                  pl.BlockSpec((tm, LANES), pos)],
        out_specs=(pl.BlockSpec((tm, W), lambda i: (i, 0)),
                   pl.BlockSpec((tm, W), lambda i: (i, 0)),
                   pl.BlockSpec((tm, MLA_HEADS * MLA_V_DIM), lambda i: (i, 0))),
        compiler_params=_cparams(("parallel",)),
        name="mla_prep",
    )(proj, qa_g, kva_g, w_uq, w_ukv, q_g, k_g, cos_t, sina_t, sinb_t)


def _mla_attn_body(q_ref, k_ref, v_ref, o_ref, m, l, acc, *, tq):
    qi = pl.program_id(2)
    q = q_ref[...]
    _softmax_init(m, l, acc)

    def block_rows(j):
        return pl.ds(pl.multiple_of(j * tq, tq), tq)

    def scores(j):
        return lax.dot_general(q, k_ref[block_rows(j), :], _NT, preferred_element_type=F32)

    def update(j, s):
        _softmax_step(s, m, l, acc, v_ref[block_rows(j), :])

    _sweep_far_blocks(jnp.maximum(qi - 1, 0), scores, update)
    causal = (lax.broadcasted_iota(I32, (tq, tq), 1) <= lax.broadcasted_iota(I32, (tq, tq), 0))

    @pl.when(qi >= 1)
    def _():
        s_prev = scores(qi - 1)
        s_diag = scores(qi)
        update(qi - 1, s_prev)
        update(qi, jnp.where(causal, s_diag, NEG_BIG))

    @pl.when(qi == 0)
    def _():
        update(0, jnp.where(causal, scores(0), NEG_BIG))
    o_ref[...] = _softmax_result(l, acc).astype(BF16)


def _mla_attention(q, k, v, batch, seq, tq):
    T = q.shape[0]
    nq = seq // tq
    return pl.pallas_call(
        functools.partial(_mla_attn_body, tq=tq),
        out_shape=jax.ShapeDtypeStruct((T, BRANCH_WIDTH), BF16),
        grid=(batch, MLA_HEADS, nq),
        in_specs=[pl.BlockSpec((tq, MLA_QK_PAD), lambda b, h, i: (b * nq + i, h)),
                  pl.BlockSpec((seq, MLA_QK_PAD), lambda b, h, i: (b, h)),
                  pl.BlockSpec((seq, MLA_V_DIM), lambda b, h, i: (b, h))],
        out_specs=pl.BlockSpec((tq, MLA_V_DIM), lambda b, h, i: (b * nq + i, h)),
        scratch_shapes=[pltpu.VMEM((tq, LANES), F32), pltpu.VMEM((tq, LANES), F32),
                        pltpu.VMEM((tq, MLA_V_DIM), F32)],
        compiler_params=_cparams(("parallel", "parallel", "arbitrary")),
        name="mla_attn",
    )(q, k, v)


def _merge_body(oa_ref, ob_ref, oc_ref, wb_ref, ga_ref, gb_ref, gc_ref, bg_ref, o_ref):
    acc = None
    for b, (o_r, g_r) in enumerate(((oa_ref, ga_ref), (ob_ref, gb_ref), (oc_ref, gc_ref))):
        gate = jax.nn.sigmoid(g_r[...].astype(F32) + bg_ref[b:b + 1, :])
        t = gate * jnp.dot(o_r[...], wb_ref[0, b], preferred_element_type=F32)
        acc = t if acc is None else acc + t
    o_ref[...] = acc.astype(BF16)


def _branch_merge(o_a, o_b, o_c, w_branch_all, layer, proj, b_gate, d_model):
    T = o_a.shape[0]
    tm, tn = _tile(T, 1024), _tile(d_model, 512)
    g0 = (QKV_W + MLA_IN_W) // tn
    nj = d_model // tn
    o_spec = pl.BlockSpec((tm, BRANCH_WIDTH), lambda i, j: (i, 0))
    gate_spec = lambda b: pl.BlockSpec((tm, tn), lambda i, j: (i, g0 + b * nj + j))
    return pl.pallas_call(
        _merge_body,
        out_shape=jax.ShapeDtypeStruct((T, d_model), BF16),
        grid=(T // tm, nj),
        in_specs=[o_spec, o_spec, o_spec,
                  pl.BlockSpec((1, N_BRANCH, BRANCH_WIDTH, tn), lambda i, j: (layer, 0, 0, j)),
                  gate_spec(0), gate_spec(1), gate_spec(2),
                  pl.BlockSpec((N_BRANCH, tn), lambda i, j: (0, j))],
        out_specs=pl.BlockSpec((tm, tn), lambda i, j: (i, j)),
        compiler_params=_cparams(("parallel", "arbitrary")),
        name="branch_merge",
    )(o_a, o_b, o_c, w_branch_all, proj, proj, proj, b_gate)


def _out_proj_body(a_ref, w_ref, r_ref, o_ref):
    o_ref[...] = r_ref[...] + jnp.dot(a_ref[...], w_ref[0], preferred_element_type=F32)


def _out_proj(merged, w_out_all, layer, x):
    T, K = merged.shape
    N = w_out_all.shape[2]
    tm, tn = _tile(T, 1024), _tile(N, 1024)
    return pl.pallas_call(
        _out_proj_body,
        out_shape=jax.ShapeDtypeStruct((T, N), F32),
        grid=(T // tm, N // tn),
        in_specs=[pl.BlockSpec((tm, K), lambda i, j: (i, 0)),
                  pl.BlockSpec((1, K, tn), lambda i, j: (layer, 0, j)),
                  pl.BlockSpec((tm, tn), lambda i, j: (i, j))],
        out_specs=pl.BlockSpec((tm, tn), lambda i, j: (i, j)),
        compiler_params=_cparams(("parallel", "arbitrary")),
        name="out_proj",
    )(merged, w_out_all, x)


def _route_body(x_ref, g_ref, wr_ref, br_ref, comb_ref, meta_ref, cnt_ref, carry_s, *, tm):
    i = pl.program_id(0)

    @pl.when(i == 0)
    def _():
        carry_s[...] = jnp.zeros(carry_s.shape, F32)

    x = x_ref[...]
    h = x * lax.rsqrt(jnp.mean(x * x, axis=-1, keepdims=True) + EPS) * g_ref[...]
    logits = jnp.dot(h, wr_ref[...], preferred_element_type=F32,
                     precision=lax.Precision.HIGHEST) + br_ref[...]
    lane = lax.broadcasted_iota(I32, (tm, LANES), 1)
    big = jnp.int32(LANES)

    gl = jnp.where(lane < N_GROUPS, logits, -jnp.inf)
    g_max = jnp.max(gl, axis=-1, keepdims=True)
    g_idx = jnp.min(jnp.where(gl == g_max, lane, big), axis=-1, keepdims=True)
    g_val = 1.0 / jnp.sum(jnp.exp(gl - g_max), axis=-1, keepdims=True)

    e_lo = N_GROUPS + g_idx * EXPERTS_PER_GROUP
    el = jnp.where(jnp.logical_and(lane >= e_lo, lane < e_lo + EXPERTS_PER_GROUP), logits, -jnp.inf)
    e1 = jnp.max(el, axis=-1, keepdims=True)
    i1 = jnp.min(jnp.where(el == e1, lane, big), axis=-1, keepdims=True)
    el2 = jnp.where(lane == i1, -jnp.inf, el)
    e2 = jnp.max(el2, axis=-1, keepdims=True)
    i2 = jnp.min(jnp.where(el2 == e2, lane, big), axis=-1, keepdims=True)
    t = jnp.exp(e2 - e1)
    w1 = g_val / (1.0 + t)
    w2 = g_val * t / (1.0 + t)
    comb_ref[...] = jnp.where(lane == i1, w1, 0.0) + jnp.where(lane == i2, w2, 0.0)

    onehot = jnp.where(lane == g_idx, 1.0, 0.0)
    r = lax.broadcasted_iota(I32, (tm, tm), 0)
    c = lax.broadcasted_iota(I32, (tm, tm), 1)
    before = jnp.where(c < r, 1.0, 0.0).astype(BF16)
    prefix = jnp.dot(before, onehot.astype(BF16), preferred_element_type=F32) + carry_s[...]
    rank = jnp.sum(jnp.where(lane == g_idx, prefix, 0.0), axis=-1, keepdims=True)
    carry_s[...] += jnp.sum(onehot, axis=0, keepdims=True)
    meta_ref[...] = jnp.where(lane == 0, g_idx, jnp.where(lane == 1, rank.astype(I32), 0))
    cnt_ref[...] = jnp.broadcast_to(carry_s[...], cnt_ref.shape).astype(I32)


def _moe_route(x, g, w_r, b_r):
    T, D = x.shape
    tm = _tile(T, 512)
    return pl.pallas_call(
        functools.partial(_route_body, tm=tm),
        out_shape=(jax.ShapeDtypeStruct((T, LANES), F32),
                   jax.ShapeDtypeStruct((T, LANES), I32), jax.ShapeDtypeStruct((8, LANES), I32)),
        grid=(T // tm,),
        in_specs=[pl.BlockSpec((tm, D), lambda i: (i, 0)),
                  pl.BlockSpec((1, D), lambda i: (0, 0)),
                  pl.BlockSpec((D, LANES), lambda i: (0, 0)),
                  pl.BlockSpec((1, LANES), lambda i: (0, 0))],
        out_specs=(pl.BlockSpec((tm, LANES), lambda i: (i, 0)),
                   pl.BlockSpec((tm, LANES), lambda i: (i, 0)),
                   pl.BlockSpec((8, LANES), lambda i: (0, 0))),
        scratch_shapes=[pltpu.VMEM((1, LANES), F32)],
        compiler_params=_cparams(("arbitrary",)),
        name="moe_route",
    )(x, g.reshape(1, D), w_r, b_r)


def _start_and_wait_rows(row_copy, n_rows):
    def issue(r, carry):
        row_copy(r).start()
        return carry
    lax.fori_loop(0, n_rows, issue, 0, unroll=ROW_DMA_UNROLL)

    def drain(r, carry):
        row_copy(r).wait()
        return carry
    lax.fori_loop(0, n_rows, drain, 0, unroll=ROW_DMA_UNROLL)


def _dispatch_body(dest_ref, x_ref, g_ref, comb_ref, xs_in, xs_ref, row_buf, sem, *, tm):
    del xs_in
    d_model = x_ref.shape[1]
    x = x_ref[...]
    row_buf[:, :d_model] = x * lax.rsqrt(jnp.mean(x * x, axis=-1, keepdims=True) + EPS) * g_ref[...]
    row_buf[:, d_model:] = comb_ref[...]
    base = pl.program_id(0) * tm

    def row_copy(r):
        return pltpu.make_async_copy(row_buf.at[pl.ds(r, 1)], xs_ref.at[pl.ds(dest_ref[base + r], 1)], sem.at[0])
    _start_and_wait_rows(row_copy, tm)


def _moe_dispatch(dest, x, g, comb, t_pad):
    T, D = x.shape
    W = D + LANES
    tm = _tile(T, ROW_DMA_TILE)
    return pl.pallas_call(
        functools.partial(_dispatch_body, tm=tm),
        out_shape=jax.ShapeDtypeStruct((t_pad, W), F32),
        grid_spec=pltpu.PrefetchScalarGridSpec(
            num_scalar_prefetch=1, grid=(T // tm,),
            in_specs=[pl.BlockSpec((tm, D), lambda i, d: (i, 0)),
                      pl.BlockSpec((1, D), lambda i, d: (0, 0)),
                      pl.BlockSpec((tm, LANES), lambda i, d: (i, 0)),
                      pl.BlockSpec(memory_space=pl.ANY)],
            out_specs=pl.BlockSpec(memory_space=pl.ANY),
            scratch_shapes=[pltpu.VMEM((tm, W), F32), pltpu.SemaphoreType.DMA((1,))]),
        input_output_aliases={4: 0},
        compiler_params=_cparams(("arbitrary",)),
        name="moe_dispatch",
    )(dest, x, g.reshape(1, D), comb, jnp.zeros((t_pad, W), F32))


def _gather_rows_into(idx_ref, table_ref, dst_ref, sem, n_rows):
    base = pl.program_id(0) * n_rows

    def row_copy(r):
        return pltpu.make_async_copy(table_ref.at[pl.ds(idx_ref[base + r], 1)], dst_ref.at[pl.ds(r, 1)], sem.at[0])
    _start_and_wait_rows(row_copy, n_rows)


def _experts_body(tg_ref, nt_ref, xs_ref, wgu_ref, wd_ref, y_ref, *, d_expert):
    i, e = pl.program_id(0), pl.program_id(1)
    valid = i < nt_ref[0]
    d_model = y_ref.shape[1]

    @pl.when(valid)
    def _():
        x = xs_ref[:, :d_model].astype(BF16)
        gu = jnp.dot(x, wgu_ref[0], preferred_element_type=F32)
        comb = xs_ref[:, d_model:]
        lane = lax.broadcasted_iota(I32, comb.shape, 1)
        sel = N_GROUPS + tg_ref[i] * EXPERTS_PER_GROUP + e
        c = jnp.sum(jnp.where(lane == sel, comb, 0.0), axis=-1, keepdims=True)
        a = (jax.nn.silu(gu[:, :d_expert]) * gu[:, d_expert:] * c).astype(BF16)
        y = jnp.dot(a, wd_ref[0], preferred_element_type=F32)

        @pl.when(e == 0)
        def _():
            y_ref[...] = y

        @pl.when(e > 0)
        def _():
            y_ref[...] += y

    @pl.when(jnp.logical_and(jnp.logical_not(valid), e == 0))
    def _():
        y_ref[...] = jnp.zeros(y_ref.shape, F32)


def _moe_experts(tile_group, n_tiles, xs, w_gu, w_down, layer, tg):
    t_pad = xs.shape[0]
    d_expert, D = w_down.shape[1:]
    expert = lambda i, e, tgr, ntr: (layer * N_EXPERTS + tgr[i] * EXPERTS_PER_GROUP + e, 0, 0)
    rows = lambda i, e, tgr, ntr: (i, 0)
    return pl.pallas_call(
        functools.partial(_experts_body, d_expert=d_expert),
        out_shape=jax.ShapeDtypeStruct((t_pad, D), F32),
        grid_spec=pltpu.PrefetchScalarGridSpec(
            num_scalar_prefetch=2, grid=(t_pad // tg, EXPERTS_PER_GROUP),
            in_specs=[pl.BlockSpec((tg, D + LANES), rows),
                      pl.BlockSpec((1, D, 2 * d_expert), expert),
                      pl.BlockSpec((1, d_expert, D), expert)],
            out_specs=pl.BlockSpec((tg, D), rows)),
        compiler_params=_cparams(("parallel", "arbitrary")),
        name="moe_experts",
    )(tile_group, n_tiles, xs, w_gu, w_down)


def _combine_body(dest_ref, x_ref, ys_ref, o_ref, buf, sem, *, tm):
    _gather_rows_into(dest_ref, ys_ref, buf, sem, tm)
    o_ref[...] = x_ref[...] + buf[...]


def _moe_combine(dest, x, ys):
    T, D = x.shape
    tm = _tile(T, ROW_DMA_TILE)
    return pl.pallas_call(
        functools.partial(_combine_body, tm=tm),
        out_shape=jax.ShapeDtypeStruct((T, D), F32),
        grid_spec=pltpu.PrefetchScalarGridSpec(
            num_scalar_prefetch=1, grid=(T // tm,),
            in_specs=[pl.BlockSpec((tm, D), lambda i, d: (i, 0)),
                      pl.BlockSpec(memory_space=pl.ANY)],
            out_specs=pl.BlockSpec((tm, D), lambda i, d: (i, 0)),
            scratch_shapes=[pltpu.VMEM((tm, D), F32), pltpu.SemaphoreType.DMA((1,))]),
        compiler_params=_cparams(("arbitrary",)),
        name="moe_combine",
    )(dest, x, ys)


def _hier_moe(x, ln_g, w_r, b_r, w_gu, w_down, layer):
    T, D = x.shape
    tg = _tile(T, 512)
    comb, meta, counts = _moe_route(x, ln_g, w_r, b_r)
    cnt = counts[0, :N_GROUPS]
    seg_tiles = (cnt + tg - 1) // tg
    tile_end = jnp.cumsum(seg_tiles)
    seg_off = (tile_end - seg_tiles) * tg
    dest = seg_off[meta[:, 0]] + meta[:, 1]
    n_tiles_max = T // tg + N_GROUPS
    n_tiles = tile_end[-1]
    tile_ids = jnp.arange(n_tiles_max, dtype=I32)
    tile_group = jnp.searchsorted(tile_end, jnp.minimum(tile_ids, n_tiles - 1), side="right").astype(I32)
    dest = dest.astype(I32)
    xs = _moe_dispatch(dest, x, ln_g, comb, n_tiles_max * tg)
    ys = _moe_experts(tile_group, n_tiles.reshape(1).astype(I32), xs, w_gu, w_down, layer, tg)
    return _moe_combine(dest, x, ys)


def _layout_w_in(w_in):
    mla_w = MLA_Q_RANK + MLA_KV_RANK + MLA_ROPE_DIM
    w_a = jnp.pad(w_in[:, :, :QKV_W + mla_w].astype(BF16), ((0, 0), (0, 0), (0, MLA_IN_W - mla_w)))
    w_b = w_in[:, :, QKV_W + mla_w:].astype(BF16)
    return w_a, w_b


def _layout_w_uq(w_uq):
    w = w_uq.reshape(MLA_Q_RANK, MLA_HEADS, MLA_NOPE_DIM + MLA_ROPE_DIM)
    w = jnp.pad(w, ((0, 0), (0, 0), (0, MLA_QK_PAD - MLA_NOPE_DIM - MLA_ROPE_DIM)))
    return w.reshape(MLA_Q_RANK, MLA_HEADS * MLA_QK_PAD).astype(BF16)


def _pad_gain(g):
    return jnp.pad(g, (0, MLA_QK_PAD - g.shape[0])).reshape(1, MLA_QK_PAD)


def _rope_tables(seq):
    pos = jnp.arange(seq, dtype=F32)
    inv_freq = 1.0 / (ROPE_THETA ** (jnp.arange(0, MLA_ROPE_DIM, 2, dtype=F32) / MLA_ROPE_DIM))
    ang = pos[:, None] * inv_freq[None, :]
    cos, sin = jnp.cos(ang), jnp.sin(ang)
    z = jnp.zeros_like(cos)
    return (jnp.concatenate([cos, cos, z, z], axis=1),
            jnp.concatenate([-sin, z, z, z], axis=1),
            jnp.concatenate([z, sin, z, z], axis=1))


def kernel(x, rel_bias, ln1_g, w_in, diff_q_norm, diff_k_norm, diff_lambda, diff_subln_g, mla_q_a_norm, mla_w_uq, mla_kv_a_norm, mla_w_ukv, mla_q_norm, mla_k_norm, w_branch, b_gate, w_out, ln2_g, router_group_w, router_group_b, router_expert_w, router_expert_b, expert_w_gu, expert_w_down):
    B, S, D = x.shape
    depth = w_in.shape[0]
    T = B * S
    tq = _tile(S, 512)
    tq_sb = _tile(S, 256)
    cos_t, sina_t, sinb_t = _rope_tables(S)
    bias_tiles = _t5_bias_tiles(rel_bias, tq)
    w_in_a, w_in_b = _layout_w_in(w_in)
    mla_blk = QKV_W // MLA_IN_W
    w_branch_all = w_branch.astype(BF16)
    w_out_all = w_out.astype(BF16)
    w_gu_all = expert_w_gu.astype(BF16).reshape((depth * N_EXPERTS,) + expert_w_gu.shape[2:])
    w_down_all = expert_w_down.astype(BF16).reshape((depth * N_EXPERTS,) + expert_w_down.shape[2:])
    xt = x.reshape(T, D)
    for l in range(depth):
        proj = _rms_matmul(xt, ln1_g[l], w_in_a, w_in_b, l, BF16, "rms_in_proj")

        lam_init = 0.8 - 0.6 * math.exp(-0.3 * l)
        lam_tab = jnp.concatenate([diff_lambda[l], jnp.full((4, DIFF_QK_DIM), lam_init, F32)], axis=0)
        o_a = _diff_attention(proj, lam_tab,
                              jnp.tile(diff_q_norm[l], 2).reshape(1, HEAD_W),
                              jnp.tile(diff_k_norm[l], 2).reshape(1, HEAD_W),
                              diff_subln_g[l].reshape(1, HEAD_W), bias_tiles, B, S, tq)
        o_b = _sb_attention(proj, B, S, tq_sb)
        q_c, k_c, v_c = _mla_prep(proj, mla_blk, mla_q_a_norm[l].reshape(1, -1), mla_kv_a_norm[l].reshape(1, -1),
                                  _layout_w_uq(mla_w_uq[l]), mla_w_ukv[l].astype(BF16),
                                  _pad_gain(mla_q_norm[l]), _pad_gain(mla_k_norm[l]),
                                  cos_t, sina_t, sinb_t, S)
        o_c = _mla_attention(q_c, k_c, v_c, B, S, tq)

        merged = _branch_merge(o_a, o_b, o_c, w_branch_all, l, proj, b_gate[l].reshape(N_BRANCH, D), D)
        xt = _out_proj(merged, w_out_all, l, xt)

        w_r = jnp.concatenate([router_group_w[l], router_expert_w[l],
                               jnp.zeros((D, LANES - N_GROUPS - N_EXPERTS), F32)], axis=1)
        b_r = jnp.concatenate([router_group_b[l], router_expert_b[l],
                               jnp.zeros((LANES - N_GROUPS - N_EXPERTS,), F32)]).reshape(1, LANES)
        xt = _hier_moe(xt, ln2_g[l], w_r, b_r, w_gu_all, w_down_all, l)
    return xt.reshape(B, S, D)
```

```python
import functools
import math

import jax
import jax.numpy as jnp
from jax import lax
from jax.experimental import pallas as pl
from jax.experimental.pallas import tpu as pltpu

F32 = jnp.float32
BF16 = jnp.bfloat16
I32 = jnp.int32

EPS = 1e-6
LANES = 128
HEAD_W = 128
DIFF_HEADS = 8
DIFF_QK_DIM = 64
SB_HEADS = 8
SB_HEAD_DIM = 128
MLA_HEADS = 8
MLA_Q_RANK = 512
MLA_KV_RANK = 256
MLA_NOPE_DIM = 128
MLA_ROPE_DIM = 64
MLA_V_DIM = 128
MLA_QK_PAD = 256
ROPE_THETA = 10000.0
REL_BUCKETS = 32
REL_MAX_DIST = 128
N_GROUPS = 8
EXPERTS_PER_GROUP = 4
N_EXPERTS = N_GROUPS * EXPERTS_PER_GROUP
N_BRANCH = 3
BRANCH_WIDTH = 1024
QKV_W = 6 * BRANCH_WIDTH
MLA_IN_W = 1024
NEG_BIG = -1e30
LOG2_E = math.log2(math.e)
SB_SKIP_LOG = -100.0 * LOG2_E
VMEM_LIMIT = 56 * 1024 * 1024
ROW_DMA_TILE = 512
ROW_DMA_UNROLL = 8

_NT = (((1,), (1,)), ((), ()))


def _cparams(sem, vmem=VMEM_LIMIT):
    return pltpu.CompilerParams(dimension_semantics=sem, vmem_limit_bytes=vmem)


def _tile(n, target):
    t = min(n, target)
    while n % t:
        t -= 1
    return t


def _round_up(n, m):
    return -(-n // m) * m


def _rms_mm_body(x_ref, g_ref, wa_ref, wb_ref, o_ref, h_ref, *, n_a):
    j = pl.program_id(1)

    @pl.when(j == 0)
    def _():
        x = x_ref[...]
        ms = jnp.mean(x * x, axis=-1, keepdims=True)
        h_ref[...] = (x * lax.rsqrt(ms + EPS) * g_ref[...]).astype(BF16)

    @pl.when(j < n_a)
    def _():
        o_ref[...] = jnp.dot(h_ref[...], wa_ref[0], preferred_element_type=F32).astype(o_ref.dtype)

    @pl.when(j >= n_a)
    def _():
        o_ref[...] = jnp.dot(h_ref[...], wb_ref[0], preferred_element_type=F32).astype(o_ref.dtype)


def _rms_matmul(x, g, wa_all, wb_all, layer, out_dtype, name):
    T, K = x.shape
    na_cols, nb_cols = wa_all.shape[2], wb_all.shape[2]
    tm, tn = _tile(T, 1024), _tile(math.gcd(na_cols, nb_cols), 1024)
    n_a, n_b = na_cols // tn, nb_cols // tn
    return pl.pallas_call(
        functools.partial(_rms_mm_body, n_a=n_a),
        out_shape=jax.ShapeDtypeStruct((T, na_cols + nb_cols), out_dtype),
        grid=(T // tm, n_a + n_b),
        in_specs=[pl.BlockSpec((tm, K), lambda i, j: (i, 0)),
                  pl.BlockSpec((1, K), lambda i, j: (0, 0)),
                  pl.BlockSpec((1, K, tn), lambda i, j: (layer, 0, jnp.minimum(j, n_a - 1))),
                  pl.BlockSpec((1, K, tn), lambda i, j: (layer, 0, jnp.where(j < n_a, n_b - 1, j - n_a)))],
        out_specs=pl.BlockSpec((tm, tn), lambda i, j: (i, j)),
        scratch_shapes=[pltpu.VMEM((tm, K), BF16)],
        compiler_params=_cparams(("parallel", "arbitrary")),
        name=name,
    )(x, g.reshape(1, K), wa_all, wb_all)


def _rep_lanes(x, n):
    return jnp.concatenate([x] * n, axis=1)


def _softmax_step(s, m_ref, l_ref, acc_ref, v):
    n_rep = s.shape[1] // LANES
    m_prev = m_ref[...]
    m_new = jnp.maximum(m_prev, jnp.max(s, axis=-1, keepdims=True))
    alpha = jnp.exp2(m_prev - m_new)
    p = jnp.exp2(s - _rep_lanes(m_new, n_rep))
    p_sum = p[:, :LANES]
    for c in range(1, n_rep):
        p_sum = p_sum + p[:, c * LANES:(c + 1) * LANES]
    l_ref[...] = alpha * l_ref[...] + p_sum
    acc_ref[...] = alpha * acc_ref[...] + jnp.dot(p.astype(BF16), v, preferred_element_type=F32)
    m_ref[...] = m_new


def _softmax_result(l_ref, acc_ref):
    return acc_ref[...] / jnp.sum(l_ref[...], axis=-1, keepdims=True)


def _sweep_far_blocks(n_far, scores, update):
    def pair(p, carry):
        s_a = scores(2 * p)
        s_b = scores(2 * p + 1)
        update(2 * p, s_a)
        update(2 * p + 1, s_b)
        return carry
    lax.fori_loop(0, n_far // 2, pair, 0)

    @pl.when(n_far % 2 == 1)
    def _():
        update(n_far - 1, scores(n_far - 1))


def _softmax_init(m_ref, l_ref, acc_ref):
    m_ref[...] = jnp.full(m_ref.shape, NEG_BIG, F32)
    l_ref[...] = jnp.zeros(l_ref.shape, F32)
    acc_ref[...] = jnp.zeros(acc_ref.shape, F32)


def _t5_tiles_body(rb_ref, o_ref, *, tq):
    h = pl.program_id(0)
    row = lax.broadcasted_iota(I32, (tq, tq), 0)
    col = lax.broadcasted_iota(I32, (tq, tq), 1)
    max_exact = REL_BUCKETS // 2
    far = rb_ref[(REL_BUCKETS - 1) * DIFF_HEADS + h]
    for which in range(2):
        d = row - col + which * tq
        n = jnp.maximum(d, 0)
        nf = jnp.maximum(n, 1).astype(F32)
        large = max_exact + (jnp.log(nf / max_exact) / math.log(REL_MAX_DIST / max_exact)
                             * (REL_BUCKETS - max_exact)).astype(I32)
        large = jnp.minimum(large, REL_BUCKETS - 1)
        bucket = jnp.where(n < max_exact, n, large)
        val = jnp.zeros((tq, tq), F32)
        for b in range(REL_BUCKETS - 1):
            val = jnp.where(bucket == b, (rb_ref[b * DIFF_HEADS + h] - far) * LOG2_E, val)
        o_ref[0, which] = jnp.where(d >= 0, val, NEG_BIG)


def _t5_bias_tiles(rel_bias, tq):
    assert tq >= REL_MAX_DIST
    return pl.pallas_call(
        functools.partial(_t5_tiles_body, tq=tq),
        out_shape=jax.ShapeDtypeStruct((DIFF_HEADS, 2, tq, tq), F32),
        grid=(DIFF_HEADS,),
        in_specs=[pl.BlockSpec(memory_space=pltpu.SMEM)],
        out_specs=pl.BlockSpec((1, 2, tq, tq), lambda h: (h, 0, 0, 0)),
        compiler_params=_cparams(("parallel",)),
        name="t5_bias_tiles",
    )(rel_bias.reshape(-1))


def _diff_attn_body(lam_ref, qg_ref, kg_ref, sg_ref, bias_ref, q_ref, k_ref, v_ref, o_ref,
                    kn_s, m1, l1, a1, m2, l2, a2, *, tq, seq):
    qi = pl.program_id(2)
    lo = lax.broadcasted_iota(I32, (1, HEAD_W), 1) < DIFF_QK_DIM
    in_lo_r = lax.broadcasted_iota(I32, (HEAD_W, HEAD_W), 0) < DIFF_QK_DIM
    in_lo_c = lax.broadcasted_iota(I32, (HEAD_W, HEAD_W), 1) < DIFF_QK_DIM
    same_half = jnp.where(in_lo_r == in_lo_c, 1.0, 0.0).astype(BF16)
    same_half2 = jnp.concatenate([same_half, same_half], axis=0)

    def half_rms(x, g):
        x2 = x * x
        hi = x2.astype(BF16)
        lo2 = (x2 - hi.astype(F32)).astype(BF16)
        ss = jnp.dot(jnp.concatenate([hi, lo2], axis=1), same_half2, preferred_element_type=F32)
        return x * lax.rsqrt(ss * (1.0 / DIFF_QK_DIM) + EPS) * g

    @pl.when(qi == 0)
    def _():
        def norm_chunk(c, carry):
            rows = pl.ds(pl.multiple_of(c * tq, tq), tq)
            kn_s[rows, :] = half_rms(k_ref[rows, :].astype(F32), kg_ref[...]).astype(BF16)
            return carry
        lax.fori_loop(0, seq // tq, norm_chunk, 0, unroll=True)

    qn = half_rms(q_ref[...].astype(F32), qg_ref[...]) * (DIFF_QK_DIM ** -0.5 * LOG2_E)
    q1 = jnp.where(lo, qn, 0.0).astype(BF16)
    q2 = jnp.where(lo, 0.0, qn).astype(BF16)

    _softmax_init(m1, l1, a1)
    _softmax_init(m2, l2, a2)

    def block_rows(j):
        return pl.ds(pl.multiple_of(j * tq, tq), tq)

    def scores(j):
        kb = kn_s[block_rows(j), :]
        return (lax.dot_general(q1, kb, _NT, preferred_element_type=F32),
                lax.dot_general(q2, kb, _NT, preferred_element_type=F32))

    def update(j, s, bias=None):
        s1, s2 = s if bias is None else (s[0] + bias, s[1] + bias)
        vb = v_ref[block_rows(j), :]
        _softmax_step(s1, m1, l1, a1, vb)
        _softmax_step(s2, m2, l2, a2, vb)

    _sweep_far_blocks(jnp.maximum(qi - 1, 0), scores, update)

    @pl.when(qi >= 1)
    def _():
        s_prev = scores(qi - 1)
        s_diag = scores(qi)
        update(qi - 1, s_prev, bias_ref[0, 1])
        update(qi, s_diag, bias_ref[0, 0])

    @pl.when(qi == 0)
    def _():
        update(0, scores(0), bias_ref[0, 0])

    lp = lam_ref[...]
    lam_init = lp[4:5, 0:1]
    lam = (jnp.exp(jnp.sum(lp[0:1] * lp[1:2], axis=-1, keepdims=True))
           - jnp.exp(jnp.sum(lp[2:3] * lp[3:4], axis=-1, keepdims=True)) + lam_init)
    o = _softmax_result(l1, a1) - lam * _softmax_result(l2, a2)
    ms = jnp.mean(o * o, axis=-1, keepdims=True)
    o = o * lax.rsqrt(ms + EPS) * sg_ref[...] * (1.0 - lam_init)
    o_ref[...] = o.astype(BF16)


def _diff_attention(proj, lam_tab, q_gain, k_gain, sub_gain, bias_tiles, batch, seq, tq):
    T = proj.shape[0]
    nq = seq // tq
    H = DIFF_HEADS
    small = lambda b, h, i: (0, 0)
    return pl.pallas_call(
        functools.partial(_diff_attn_body, tq=tq, seq=seq),
        out_shape=jax.ShapeDtypeStruct((T, BRANCH_WIDTH), BF16),
        grid=(batch, H, nq),
        in_specs=[pl.BlockSpec((8, DIFF_QK_DIM), small),
                  pl.BlockSpec((1, HEAD_W), small),
                  pl.BlockSpec((1, HEAD_W), small),
                  pl.BlockSpec((1, HEAD_W), small),
                  pl.BlockSpec((1, 2, tq, tq), lambda b, h, i: (h, 0, 0, 0)),
                  pl.BlockSpec((tq, HEAD_W), lambda b, h, i: (b * nq + i, h)),
                  pl.BlockSpec((seq, HEAD_W), lambda b, h, i: (b, H + h)),
                  pl.BlockSpec((seq, HEAD_W), lambda b, h, i: (b, 2 * H + h))],
        out_specs=pl.BlockSpec((tq, HEAD_W), lambda b, h, i: (b * nq + i, h)),
        scratch_shapes=[pltpu.VMEM((seq, HEAD_W), BF16),
                        pltpu.VMEM((tq, LANES), F32), pltpu.VMEM((tq, LANES), F32), pltpu.VMEM((tq, HEAD_W), F32),
                        pltpu.VMEM((tq, LANES), F32), pltpu.VMEM((tq, LANES), F32), pltpu.VMEM((tq, HEAD_W), F32)],
        compiler_params=_cparams(("parallel", "parallel", "arbitrary")),
        name="diff_attn",
    )(lam_tab, q_gain, k_gain, sub_gain, bias_tiles, proj, proj, proj)


def _sb_attn_body(q_ref, k_ref, v_ref, o_ref, *scratch, sb, n_sub):
    qi = pl.program_id(2)
    row = lax.broadcasted_iota(I32, (sb, sb), 0)
    col = lax.broadcasted_iota(I32, (sb, sb), 1)
    tri = jnp.where(row >= col, 1.0, 0.0).astype(BF16)
    tri2 = jnp.concatenate([tri, tri], axis=0)
    strict = col < row
    n_rep = sb // LANES

    c_refs, acc_refs = scratch[:n_sub], scratch[n_sub:]
    for ref in scratch:
        ref[...] = jnp.zeros(ref.shape, F32)

    def sweep(it, masked):
        subs = range(n_sub)
        blks = [qi * n_sub + a - it for a in subs]
        rows = [pl.ds(pl.multiple_of(jnp.maximum(b, 0) * sb, sb), sb) for b in blks]
        zs = [lax.dot_general(q_ref[a * sb:(a + 1) * sb, :], k_ref[rows[a], :], _NT,
                              preferred_element_type=F32) * (SB_HEAD_DIM ** -0.5 * LOG2_E) for a in subs]
        log_fails, cums = [], []
        for a in subs:
            log_fail = -(jnp.maximum(zs[a], 0.0) + jnp.log2(1.0 + jnp.exp2(-jnp.abs(zs[a]))))
            if masked:
                log_fail = jnp.where(strict, log_fail, 0.0)
            hi = log_fail.astype(BF16)
            lo = (log_fail - hi.astype(F32)).astype(BF16)
            cums.append(jnp.dot(jnp.concatenate([hi, lo], axis=1), tri2, preferred_element_type=F32))
            log_fails.append(log_fail)
        c_max = jnp.float32(-jnp.inf)
        for a in subs:
            c = c_refs[a][...]
            dead = jnp.where(blks[a] >= 0, 0.0, NEG_BIG)
            w = jnp.exp2(zs[a] + cums[a] + _rep_lanes(c + dead, n_rep))
            if masked:
                w = jnp.where(strict, w, 0.0)
            acc_refs[a][...] += jnp.dot(w.astype(BF16), v_ref[rows[a], :], preferred_element_type=F32)
            c_new = c + jnp.sum(log_fails[a], axis=-1, keepdims=True)
            c_refs[a][...] = c_new
            c_max = jnp.maximum(c_max, jnp.where(blks[a] >= 1, jnp.max(c_new), -jnp.inf))
        return c_max

    c_max = sweep(0, True)

    def cond(carry):
        return carry[1] > SB_SKIP_LOG

    def body(carry):
        return carry[0] + 1, sweep(carry[0], False)

    lax.while_loop(cond, body, (jnp.int32(1), c_max))
    for a in range(n_sub):
        o_ref[a * sb:(a + 1) * sb, :] = acc_refs[a][...].astype(BF16)


def _sb_attention(proj, batch, seq, sb):
    T = proj.shape[0]
    n_sub = _tile(seq // sb, 4)
    tq = sb * n_sub
    nq = seq // tq
    H = SB_HEADS
    base = 3 * DIFF_HEADS
    return pl.pallas_call(
        functools.partial(_sb_attn_body, sb=sb, n_sub=n_sub),
        out_shape=jax.ShapeDtypeStruct((T, BRANCH_WIDTH), BF16),
        grid=(batch, H, nq),
        in_specs=[pl.BlockSpec((tq, HEAD_W), lambda b, h, i: (b * nq + i, base + h)),
                  pl.BlockSpec((seq, HEAD_W), lambda b, h, i: (b, base + H + h)),
                  pl.BlockSpec((seq, HEAD_W), lambda b, h, i: (b, base + 2 * H + h))],
        out_specs=pl.BlockSpec((tq, HEAD_W), lambda b, h, i: (b * nq + i, h)),
        scratch_shapes=([pltpu.VMEM((sb, LANES), F32)] * n_sub + [pltpu.VMEM((sb, HEAD_W), F32)] * n_sub),
        compiler_params=_cparams(("parallel", "parallel", "arbitrary")),
        name="sb_attn",
    )(proj, proj, proj)


def _mla_prep_body(pm_ref, qa_ref, kva_ref, wuq_ref, wukv_ref, qg_ref, kg_ref,
                   cos_ref, sina_ref, sinb_ref, q_ref, k_ref, v_ref):
    x = pm_ref[...].astype(F32)

    def rms(t, g):
        return t * lax.rsqrt(jnp.mean(t * t, axis=-1, keepdims=True) + EPS) * g

    cq = rms(x[:, :MLA_Q_RANK], qa_ref[...]).astype(BF16)
    ckv = rms(x[:, MLA_Q_RANK:MLA_Q_RANK + MLA_KV_RANK], kva_ref[...]).astype(BF16)
    kr = x[:, MLA_Q_RANK + MLA_KV_RANK:MLA_Q_RANK + MLA_KV_RANK + LANES]
    q = jnp.dot(cq, wuq_ref[...], preferred_element_type=F32)
    kv = jnp.dot(ckv, wukv_ref[...], preferred_element_type=F32)
    cos, sina, sinb = cos_ref[...], sina_ref[...], sinb_ref[...]
    half = MLA_ROPE_DIM // 2

    def rope(pe):
        return pe * cos + pltpu.roll(pe, LANES - half, 1) * sina + pltpu.roll(pe, half, 1) * sinb

    ones2 = jnp.ones((2 * LANES, LANES), BF16)

    def lane_sum(t):
        hi = t.astype(BF16)
        lo = (t - hi.astype(F32)).astype(BF16)
        return jnp.dot(jnp.concatenate([hi, lo], axis=1), ones2, preferred_element_type=F32)

    k_pe = rope(kr)
    k_pe_sq = k_pe * k_pe
    qk_dim = MLA_NOPE_DIM + MLA_ROPE_DIM
    qg, kg = qg_ref[...], kg_ref[...]
    for h in range(MLA_HEADS):
        o = h * MLA_QK_PAD
        q_n = q[:, o:o + MLA_NOPE_DIM]
        q_pe = rope(q[:, o + MLA_NOPE_DIM:o + MLA_QK_PAD])
        ss = lane_sum(q_n * q_n + q_pe * q_pe)
        r = lax.rsqrt(ss * (1.0 / qk_dim) + EPS) * (qk_dim ** -0.5 * LOG2_E)
        q_ref[:, o:o + MLA_NOPE_DIM] = (q_n * r * qg[:, :MLA_NOPE_DIM]).astype(BF16)
        q_ref[:, o + MLA_NOPE_DIM:o + MLA_QK_PAD] = (q_pe * r * qg[:, MLA_NOPE_DIM:]).astype(BF16)
        k_n = kv[:, o:o + MLA_NOPE_DIM]
        ssk = lane_sum(k_n * k_n + k_pe_sq)
        rk = lax.rsqrt(ssk * (1.0 / qk_dim) + EPS)
        k_ref[:, o:o + MLA_NOPE_DIM] = (k_n * rk * kg[:, :MLA_NOPE_DIM]).astype(BF16)
        k_ref[:, o + MLA_NOPE_DIM:o + MLA_QK_PAD] = (k_pe * rk * kg[:, MLA_NOPE_DIM:]).astype(BF16)
        v_ref[:, h * MLA_V_DIM:(h + 1) * MLA_V_DIM] = kv[:, o + MLA_NOPE_DIM:o + MLA_QK_PAD].astype(BF16)


def _mla_prep(proj, mla_blk, qa_g, kva_g, w_uq, w_ukv, q_g, k_g, cos_t, sina_t, sinb_t, seq):
    T = proj.shape[0]
    tm = _tile(seq, 512)
    ns = seq // tm
    W = MLA_HEADS * MLA_QK_PAD
    const = lambda i: (0, 0)
    pos = lambda i: (i % ns, 0)
    return pl.pallas_call(
        _mla_prep_body,
        out_shape=(jax.ShapeDtypeStruct((T, W), BF16), jax.ShapeDtypeStruct((T, W), BF16),
                   jax.ShapeDtypeStruct((T, MLA_HEADS * MLA_V_DIM), BF16)),
        grid=(T // tm,),
        in_specs=[pl.BlockSpec((tm, MLA_IN_W), lambda i: (i, mla_blk)),
                  pl.BlockSpec((1, MLA_Q_RANK), const),
                  pl.BlockSpec((1, MLA_KV_RANK), const),
                  pl.BlockSpec((MLA_Q_RANK, W), const),
                  pl.BlockSpec((MLA_KV_RANK, W), const),
                  pl.BlockSpec((1, MLA_QK_PAD), const),
                  pl.BlockSpec((1, MLA_QK_PAD), const),
                  pl.BlockSpec((tm, LANES), pos),
                  pl.BlockSpec((tm, LANES), pos),
                  pl.BlockSpec((tm, LANES), pos)],
        out_specs=(pl.BlockSpec((tm, W), lambda i: (i, 0)),
                   pl.BlockSpec((tm, W), lambda i: (i, 0)),
                   pl.BlockSpec((tm, MLA_HEADS * MLA_V_DIM), lambda i: (i, 0))),
        compiler_params=_cparams(("parallel",)),
        name="mla_prep",
    )(proj, qa_g, kva_g, w_uq, w_ukv, q_g, k_g, cos_t, sina_t, sinb_t)


def _mla_attn_body(q_ref, k_ref, v_ref, o_ref, m, l, acc, *, tq):
    qi = pl.program_id(2)
    q = q_ref[...]
    _softmax_init(m, l, acc)

    def block_rows(j):
        return pl.ds(pl.multiple_of(j * tq, tq), tq)

    def scores(j):
        return lax.dot_general(q, k_ref[block_rows(j), :], _NT, preferred_element_type=F32)

    def update(j, s):
        _softmax_step(s, m, l, acc, v_ref[block_rows(j), :])

    _sweep_far_blocks(jnp.maximum(qi - 1, 0), scores, update)
    causal = (lax.broadcasted_iota(I32, (tq, tq), 1) <= lax.broadcasted_iota(I32, (tq, tq), 0))

    @pl.when(qi >= 1)
    def _():
        s_prev = scores(qi - 1)
        s_diag = scores(qi)
        update(qi - 1, s_prev)
        update(qi, jnp.where(causal, s_diag, NEG_BIG))

    @pl.when(qi == 0)
    def _():
        update(0, jnp.where(causal, scores(0), NEG_BIG))
    o_ref[...] = _softmax_result(l, acc).astype(BF16)


def _mla_attention(q, k, v, batch, seq, tq):
    T = q.shape[0]
    nq = seq // tq
    return pl.pallas_call(
        functools.partial(_mla_attn_body, tq=tq),
        out_shape=jax.ShapeDtypeStruct((T, BRANCH_WIDTH), BF16),
        grid=(batch, MLA_HEADS, nq),
        in_specs=[pl.BlockSpec((tq, MLA_QK_PAD), lambda b, h, i: (b * nq + i, h)),
                  pl.BlockSpec((seq, MLA_QK_PAD), lambda b, h, i: (b, h)),
                  pl.BlockSpec((seq, MLA_V_DIM), lambda b, h, i: (b, h))],
        out_specs=pl.BlockSpec((tq, MLA_V_DIM), lambda b, h, i: (b * nq + i, h)),
        scratch_shapes=[pltpu.VMEM((tq, LANES), F32), pltpu.VMEM((tq, LANES), F32),
                        pltpu.VMEM((tq, MLA_V_DIM), F32)],
        compiler_params=_cparams(("parallel", "parallel", "arbitrary")),
        name="mla_attn",
    )(q, k, v)


def _merge_body(oa_ref, ob_ref, oc_ref, wb_ref, ga_ref, gb_ref, gc_ref, bg_ref, o_ref):
    acc = None
    for b, (o_r, g_r) in enumerate(((oa_ref, ga_ref), (ob_ref, gb_ref), (oc_ref, gc_ref))):
        gate = jax.nn.sigmoid(g_r[...].astype(F32) + bg_ref[b:b + 1, :])
        t = gate * jnp.dot(o_r[...], wb_ref[0, b], preferred_element_type=F32)
        acc = t if acc is None else acc + t
    o_ref[...] = acc.astype(BF16)


def _branch_merge(o_a, o_b, o_c, w_branch_all, layer, proj, b_gate, d_model):
    T = o_a.shape[0]
    tm, tn = _tile(T, 1024), _tile(d_model, 512)
    g0 = (QKV_W + MLA_IN_W) // tn
    nj = d_model // tn
    o_spec = pl.BlockSpec((tm, BRANCH_WIDTH), lambda i, j: (i, 0))
    gate_spec = lambda b: pl.BlockSpec((tm, tn), lambda i, j: (i, g0 + b * nj + j))
    return pl.pallas_call(
        _merge_body,
        out_shape=jax.ShapeDtypeStruct((T, d_model), BF16),
        grid=(T // tm, nj),
        in_specs=[o_spec, o_spec, o_spec,
                  pl.BlockSpec((1, N_BRANCH, BRANCH_WIDTH, tn), lambda i, j: (layer, 0, 0, j)),
                  gate_spec(0), gate_spec(1), gate_spec(2),
                  pl.BlockSpec((N_BRANCH, tn), lambda i, j: (0, j))],
        out_specs=pl.BlockSpec((tm, tn), lambda i, j: (i, j)),
        compiler_params=_cparams(("parallel", "arbitrary")),
        name="branch_merge",
    )(o_a, o_b, o_c, w_branch_all, proj, proj, proj, b_gate)


def _out_proj_body(a_ref, w_ref, r_ref, o_ref):
    o_ref[...] = r_ref[...] + jnp.dot(a_ref[...], w_ref[0], preferred_element_type=F32)


def _out_proj(merged, w_out_all, layer, x):
    T, K = merged.shape
    N = w_out_all.shape[2]
    tm, tn = _tile(T, 1024), _tile(N, 1024)
    return pl.pallas_call(
        _out_proj_body,
        out_shape=jax.ShapeDtypeStruct((T, N), F32),
        grid=(T // tm, N // tn),
        in_specs=[pl.BlockSpec((tm, K), lambda i, j: (i, 0)),
                  pl.BlockSpec((1, K, tn), lambda i, j: (layer, 0, j)),
                  pl.BlockSpec((tm, tn), lambda i, j: (i, j))],
        out_specs=pl.BlockSpec((tm, tn), lambda i, j: (i, j)),
        compiler_params=_cparams(("parallel", "arbitrary")),
        name="out_proj",
    )(merged, w_out_all, x)


def _route_body(x_ref, g_ref, wr_ref, br_ref, comb_ref, meta_ref, cnt_ref, carry_s, *, tm):
    i = pl.program_id(0)

    @pl.when(i == 0)
    def _():
        carry_s[...] = jnp.zeros(carry_s.shape, F32)

    x = x_ref[...]
    h = x * lax.rsqrt(jnp.mean(x * x, axis=-1, keepdims=True) + EPS) * g_ref[...]
    logits = jnp.dot(h, wr_ref[...], preferred_element_type=F32,
                     precision=lax.Precision.HIGHEST) + br_ref[...]
    lane = lax.broadcasted_iota(I32, (tm, LANES), 1)
    big = jnp.int32(LANES)

    gl = jnp.where(lane < N_GROUPS, logits, -jnp.inf)
    g_max = jnp.max(gl, axis=-1, keepdims=True)
    g_idx = jnp.min(jnp.where(gl == g_max, lane, big), axis=-1, keepdims=True)
    g_val = 1.0 / jnp.sum(jnp.exp(gl - g_max), axis=-1, keepdims=True)

    e_lo = N_GROUPS + g_idx * EXPERTS_PER_GROUP
    el = jnp.where(jnp.logical_and(lane >= e_lo, lane < e_lo + EXPERTS_PER_GROUP), logits, -jnp.inf)
    e1 = jnp.max(el, axis=-1, keepdims=True)
    i1 = jnp.min(jnp.where(el == e1, lane, big), axis=-1, keepdims=True)
    el2 = jnp.where(lane == i1, -jnp.inf, el)
    e2 = jnp.max(el2, axis=-1, keepdims=True)
    i2 = jnp.min(jnp.where(el2 == e2, lane, big), axis=-1, keepdims=True)
    t = jnp.exp(e2 - e1)
    w1 = g_val / (1.0 + t)
    w2 = g_val * t / (1.0 + t)
    comb_ref[...] = jnp.where(lane == i1, w1, 0.0) + jnp.where(lane == i2, w2, 0.0)

    onehot = jnp.where(lane == g_idx, 1.0, 0.0)
    r = lax.broadcasted_iota(I32, (tm, tm), 0)
    c = lax.broadcasted_iota(I32, (tm, tm), 1)
    before = jnp.where(c < r, 1.0, 0.0).astype(BF16)
    prefix = jnp.dot(before, onehot.astype(BF16), preferred_element_type=F32) + carry_s[...]
    rank = jnp.sum(jnp.where(lane == g_idx, prefix, 0.0), axis=-1, keepdims=True)
    carry_s[...] += jnp.sum(onehot, axis=0, keepdims=True)
    meta_ref[...] = jnp.where(lane == 0, g_idx, jnp.where(lane == 1, rank.astype(I32), 0))
    cnt_ref[...] = jnp.broadcast_to(carry_s[...], cnt_ref.shape).astype(I32)


def _moe_route(x, g, w_r, b_r):
    T, D = x.shape
    tm = _tile(T, 512)
    return pl.pallas_call(
        functools.partial(_route_body, tm=tm),
        out_shape=(jax.ShapeDtypeStruct((T, LANES), F32),
                   jax.ShapeDtypeStruct((T, LANES), I32), jax.ShapeDtypeStruct((8, LANES), I32)),
        grid=(T // tm,),
        in_specs=[pl.BlockSpec((tm, D), lambda i: (i, 0)),
                  pl.BlockSpec((1, D), lambda i: (0, 0)),
                  pl.BlockSpec((D, LANES), lambda i: (0, 0)),
                  pl.BlockSpec((1, LANES), lambda i: (0, 0))],
        out_specs=(pl.BlockSpec((tm, LANES), lambda i: (i, 0)),
                   pl.BlockSpec((tm, LANES), lambda i: (i, 0)),
                   pl.BlockSpec((8, LANES), lambda i: (0, 0))),
        scratch_shapes=[pltpu.VMEM((1, LANES), F32)],
        compiler_params=_cparams(("arbitrary",)),
        name="moe_route",
    )(x, g.reshape(1, D), w_r, b_r)


def _start_and_wait_rows(row_copy, n_rows):
    def issue(r, carry):
        row_copy(r).start()
        return carry
    lax.fori_loop(0, n_rows, issue, 0, unroll=ROW_DMA_UNROLL)

    def drain(r, carry):
        row_copy(r).wait()
        return carry
    lax.fori_loop(0, n_rows, drain, 0, unroll=ROW_DMA_UNROLL)


def _dispatch_body(dest_ref, x_ref, g_ref, comb_ref, xs_in, xs_ref, row_buf, sem, *, tm):
    del xs_in
    d_model = x_ref.shape[1]
    x = x_ref[...]
    row_buf[:, :d_model] = x * lax.rsqrt(jnp.mean(x * x, axis=-1, keepdims=True) + EPS) * g_ref[...]
    row_buf[:, d_model:] = comb_ref[...]
    base = pl.program_id(0) * tm

    def row_copy(r):
        return pltpu.make_async_copy(row_buf.at[pl.ds(r, 1)], xs_ref.at[pl.ds(dest_ref[base + r], 1)], sem.at[0])
    _start_and_wait_rows(row_copy, tm)


def _moe_dispatch(dest, x, g, comb, t_pad):
    T, D = x.shape
    W = D + LANES
    tm = _tile(T, ROW_DMA_TILE)
    return pl.pallas_call(
        functools.partial(_dispatch_body, tm=tm),
        out_shape=jax.ShapeDtypeStruct((t_pad, W), F32),
        grid_spec=pltpu.PrefetchScalarGridSpec(
            num_scalar_prefetch=1, grid=(T // tm,),
            in_specs=[pl.BlockSpec((tm, D), lambda i, d: (i, 0)),
                      pl.BlockSpec((1, D), lambda i, d: (0, 0)),
                      pl.BlockSpec((tm, LANES), lambda i, d: (i, 0)),
                      pl.BlockSpec(memory_space=pl.ANY)],
            out_specs=pl.BlockSpec(memory_space=pl.ANY),
            scratch_shapes=[pltpu.VMEM((tm, W), F32), pltpu.SemaphoreType.DMA((1,))]),
        input_output_aliases={4: 0},
        compiler_params=_cparams(("arbitrary",)),
        name="moe_dispatch",
    )(dest, x, g.reshape(1, D), comb, jnp.zeros((t_pad, W), F32))


def _gather_rows_into(idx_ref, table_ref, dst_ref, sem, n_rows):
    base = pl.program_id(0) * n_rows

    def row_copy(r):
        return pltpu.make_async_copy(table_ref.at[pl.ds(idx_ref[base + r], 1)], dst_ref.at[pl.ds(r, 1)], sem.at[0])
    _start_and_wait_rows(row_copy, n_rows)


def _experts_body(tg_ref, nt_ref, xs_ref, wgu_ref, wd_ref, y_ref, *, d_expert):
    i, e = pl.program_id(0), pl.program_id(1)
    valid = i < nt_ref[0]
    d_model = y_ref.shape[1]

    @pl.when(valid)
    def _():
        x = xs_ref[:, :d_model].astype(BF16)
        gu = jnp.dot(x, wgu_ref[0], preferred_element_type=F32)
        comb = xs_ref[:, d_model:]
        lane = lax.broadcasted_iota(I32, comb.shape, 1)
        sel = N_GROUPS + tg_ref[i] * EXPERTS_PER_GROUP + e
        c = jnp.sum(jnp.where(lane == sel, comb, 0.0), axis=-1, keepdims=True)
        a = (jax.nn.silu(gu[:, :d_expert]) * gu[:, d_expert:] * c).astype(BF16)
        y = jnp.dot(a, wd_ref[0], preferred_element_type=F32)

        @pl.when(e == 0)
        def _():
            y_ref[...] = y

        @pl.when(e > 0)
        def _():
            y_ref[...] += y

    @pl.when(jnp.logical_and(jnp.logical_not(valid), e == 0))
    def _():
        y_ref[...] = jnp.zeros(y_ref.shape, F32)


def _moe_experts(tile_group, n_tiles, xs, w_gu, w_down, layer, tg):
    t_pad = xs.shape[0]
    d_expert, D = w_down.shape[1:]
    expert = lambda i, e, tgr, ntr: (layer * N_EXPERTS + tgr[i] * EXPERTS_PER_GROUP + e, 0, 0)
    rows = lambda i, e, tgr, ntr: (i, 0)
    return pl.pallas_call(
        functools.partial(_experts_body, d_expert=d_expert),
        out_shape=jax.ShapeDtypeStruct((t_pad, D), F32),
        grid_spec=pltpu.PrefetchScalarGridSpec(
            num_scalar_prefetch=2, grid=(t_pad // tg, EXPERTS_PER_GROUP),
            in_specs=[pl.BlockSpec((tg, D + LANES), rows),
                      pl.BlockSpec((1, D, 2 * d_expert), expert),
                      pl.BlockSpec((1, d_expert, D), expert)],
            out_specs=pl.BlockSpec((tg, D), rows)),
        compiler_params=_cparams(("parallel", "arbitrary")),
        name="moe_experts",
    )(tile_group, n_tiles, xs, w_gu, w_down)


def _combine_body(dest_ref, x_ref, ys_ref, o_ref, buf, sem, *, tm):
    _gather_rows_into(dest_ref, ys_ref, buf, sem, tm)
    o_ref[...] = x_ref[...] + buf[...]


def _moe_combine(dest, x, ys):
    T, D = x.shape
    tm = _tile(T, ROW_DMA_TILE)
    return pl.pallas_call(
        functools.partial(_combine_body, tm=tm),
        out_shape=jax.ShapeDtypeStruct((T, D), F32),
        grid_spec=pltpu.PrefetchScalarGridSpec(
            num_scalar_prefetch=1, grid=(T // tm,),
            in_specs=[pl.BlockSpec((tm, D), lambda i, d: (i, 0)),
                      pl.BlockSpec(memory_space=pl.ANY)],
            out_specs=pl.BlockSpec((tm, D), lambda i, d: (i, 0)),
            scratch_shapes=[pltpu.VMEM((tm, D), F32), pltpu.SemaphoreType.DMA((1,))]),
        compiler_params=_cparams(("arbitrary",)),
        name="moe_combine",
    )(dest, x, ys)


def _hier_moe(x, ln_g, w_r, b_r, w_gu, w_down, layer):
    T, D = x.shape
    tg = _tile(T, 512)
    comb, meta, counts = _moe_route(x, ln_g, w_r, b_r)
    cnt = counts[0, :N_GROUPS]
    seg_tiles = (cnt + tg - 1) // tg
    tile_end = jnp.cumsum(seg_tiles)
    seg_off = (tile_end - seg_tiles) * tg
    dest = seg_off[meta[:, 0]] + meta[:, 1]
    n_tiles_max = T // tg + N_GROUPS
    n_tiles = tile_end[-1]
    tile_ids = jnp.arange(n_tiles_max, dtype=I32)
    tile_group = jnp.searchsorted(tile_end, jnp.minimum(tile_ids, n_tiles - 1), side="right").astype(I32)
    dest = dest.astype(I32)
    xs = _moe_dispatch(dest, x, ln_g, comb, n_tiles_max * tg)
    ys = _moe_experts(tile_group, n_tiles.reshape(1).astype(I32), xs, w_gu, w_down, layer, tg)
    return _moe_combine(dest, x, ys)


def _layout_w_in(w_in):
    mla_w = MLA_Q_RANK + MLA_KV_RANK + MLA_ROPE_DIM
    w_a = jnp.pad(w_in[:, :, :QKV_W + mla_w].astype(BF16), ((0, 0), (0, 0), (0, MLA_IN_W - mla_w)))
    w_b = w_in[:, :, QKV_W + mla_w:].astype(BF16)
    return w_a, w_b


def _layout_w_uq(w_uq):
    w = w_uq.reshape(MLA_Q_RANK, MLA_HEADS, MLA_NOPE_DIM + MLA_ROPE_DIM)
    w = jnp.pad(w, ((0, 0), (0, 0), (0, MLA_QK_PAD - MLA_NOPE_DIM - MLA_ROPE_DIM)))
    return w.reshape(MLA_Q_RANK, MLA_HEADS * MLA_QK_PAD).astype(BF16)


def _pad_gain(g):
    return jnp.pad(g, (0, MLA_QK_PAD - g.shape[0])).reshape(1, MLA_QK_PAD)


def _rope_tables(seq):
    pos = jnp.arange(seq, dtype=F32)
    inv_freq = 1.0 / (ROPE_THETA ** (jnp.arange(0, MLA_ROPE_DIM, 2, dtype=F32) / MLA_ROPE_DIM))
    ang = pos[:, None] * inv_freq[None, :]
    cos, sin = jnp.cos(ang), jnp.sin(ang)
    z = jnp.zeros_like(cos)
    return (jnp.concatenate([cos, cos, z, z], axis=1),
            jnp.concatenate([-sin, z, z, z], axis=1),
            jnp.concatenate([z, sin, z, z], axis=1))


def kernel(x, rel_bias, ln1_g, w_in, diff_q_norm, diff_k_norm, diff_lambda, diff_subln_g, mla_q_a_norm, mla_w_uq, mla_kv_a_norm, mla_w_ukv, mla_q_norm, mla_k_norm, w_branch, b_gate, w_out, ln2_g, router_group_w, router_group_b, router_expert_w, router_expert_b, expert_w_gu, expert_w_down):
    B, S, D = x.shape
    depth = w_in.shape[0]
    T = B * S
    tq = _tile(S, 512)
    tq_sb = _tile(S, 256)
    cos_t, sina_t, sinb_t = _rope_tables(S)
    bias_tiles = _t5_bias_tiles(rel_bias, tq)
    w_in_a, w_in_b = _layout_w_in(w_in)
    mla_blk = QKV_W // MLA_IN_W
    w_branch_all = w_branch.astype(BF16)
    w_out_all = w_out.astype(BF16)
    w_gu_all = expert_w_gu.astype(BF16).reshape((depth * N_EXPERTS,) + expert_w_gu.shape[2:])
    w_down_all = expert_w_down.astype(BF16).reshape((depth * N_EXPERTS,) + expert_w_down.shape[2:])
    xt = x.reshape(T, D)
    for l in range(depth):
        proj = _rms_matmul(xt, ln1_g[l], w_in_a, w_in_b, l, BF16, "rms_in_proj")

        lam_init = 0.8 - 0.6 * math.exp(-0.3 * l)
        lam_tab = jnp.concatenate([diff_lambda[l], jnp.full((4, DIFF_QK_DIM), lam_init, F32)], axis=0)
        o_a = _diff_attention(proj, lam_tab,
                              jnp.tile(diff_q_norm[l], 2).reshape(1, HEAD_W),
                              jnp.tile(diff_k_norm[l], 2).reshape(1, HEAD_W),
                              diff_subln_g[l].reshape(1, HEAD_W), bias_tiles, B, S, tq)
        o_b = _sb_attention(proj, B, S, tq_sb)
        q_c, k_c, v_c = _mla_prep(proj, mla_blk, mla_q_a_norm[l].reshape(1, -1), mla_kv_a_norm[l].reshape(1, -1),
                                  _layout_w_uq(mla_w_uq[l]), mla_w_ukv[l].astype(BF16),
                                  _pad_gain(mla_q_norm[l]), _pad_gain(mla_k_norm[l]),
                                  cos_t, sina_t, sinb_t, S)
        o_c = _mla_attention(q_c, k_c, v_c, B, S, tq)

        merged = _branch_merge(o_a, o_b, o_c, w_branch_all, l, proj, b_gate[l].reshape(N_BRANCH, D), D)
        xt = _out_proj(merged, w_out_all, l, xt)

        w_r = jnp.concatenate([router_group_w[l], router_expert_w[l],
                               jnp.zeros((D, LANES - N_GROUPS - N_EXPERTS), F32)], axis=1)
        b_r = jnp.concatenate([router_group_b[l], router_expert_b[l],
                               jnp.zeros((LANES - N_GROUPS - N_EXPERTS,), F32)]).reshape(1, LANES)
        xt = _hier_moe(xt, ln2_g[l], w_r, b_r, w_gu_all, w_down_all, l)
    return xt.reshape(B, S, D)
```

```python
import functools
import math

import jax
import jax.numpy as jnp
from jax import lax
from jax.experimental import pallas as pl
from jax.experimental.pallas import tpu as pltpu

F32 = jnp.float32
BF16 = jnp.bfloat16
I32 = jnp.int32

EPS = 1e-6
LANES = 128
HEAD_W = 128
DIFF_HEADS = 8
DIFF_QK_DIM = 64
SB_HEADS = 8
SB_HEAD_DIM = 128
MLA_HEADS = 8
MLA_Q_RANK = 512
MLA_KV_RANK = 256
MLA_NOPE_DIM = 128
MLA_ROPE_DIM = 64
MLA_V_DIM = 128
MLA_QK_PAD = 256
ROPE_THETA = 10000.0
REL_BUCKETS = 32
REL_MAX_DIST = 128
N_GROUPS = 8
EXPERTS_PER_GROUP = 4
N_EXPERTS = N_GROUPS * EXPERTS_PER_GROUP
N_BRANCH = 3
BRANCH_WIDTH = 1024
QKV_W = 6 * BRANCH_WIDTH
MLA_IN_W = 1024
NEG_BIG = -1e30
LOG2_E = math.log2(math.e)
SB_SKIP_LOG = -100.0 * LOG2_E
VMEM_LIMIT = 56 * 1024 * 1024
BOUND_MARGIN = 1.02
BOUNDED_SOFTMAX_RANGE = 96.0
ROW_DMA_TILE = 512
ROW_DMA_UNROLL = 8

_NT = (((1,), (1,)), ((), ()))


def _cparams(sem, vmem=VMEM_LIMIT):
    return pltpu.CompilerParams(dimension_semantics=sem, vmem_limit_bytes=vmem)


def _tile(n, target):
    t = min(n, target)
    while n % t:
        t -= 1
    return t


def _round_up(n, m):
    return -(-n // m) * m


def _rms_mm_body(x_ref, g_ref, wa_ref, wb_ref, o_ref, h_ref, *, n_a):
    j = pl.program_id(1)

    @pl.when(j == 0)
    def _():
        x = x_ref[...]
        ms = jnp.mean(x * x, axis=-1, keepdims=True)
        h_ref[...] = (x * lax.rsqrt(ms + EPS) * g_ref[...]).astype(BF16)

    @pl.when(j < n_a)
    def _():
        o_ref[...] = jnp.dot(h_ref[...], wa_ref[0], preferred_element_type=F32).astype(o_ref.dtype)

    @pl.when(j >= n_a)
    def _():
        o_ref[...] = jnp.dot(h_ref[...], wb_ref[0], preferred_element_type=F32).astype(o_ref.dtype)


def _rms_matmul(x, g, wa_all, wb_all, layer, out_dtype, name):
    T, K = x.shape
    na_cols, nb_cols = wa_all.shape[2], wb_all.shape[2]
    tm, tn = _tile(T, 1024), _tile(math.gcd(na_cols, nb_cols), 1024)
    n_a, n_b = na_cols // tn, nb_cols // tn
    return pl.pallas_call(
        functools.partial(_rms_mm_body, n_a=n_a),
        out_shape=jax.ShapeDtypeStruct((T, na_cols + nb_cols), out_dtype),
        grid=(T // tm, n_a + n_b),
        in_specs=[pl.BlockSpec((tm, K), lambda i, j: (i, 0)),
                  pl.BlockSpec((1, K), lambda i, j: (0, 0)),
                  pl.BlockSpec((1, K, tn), lambda i, j: (layer, 0, jnp.minimum(j, n_a - 1))),
                  pl.BlockSpec((1, K, tn), lambda i, j: (layer, 0, jnp.where(j < n_a, n_b - 1, j - n_a)))],
        out_specs=pl.BlockSpec((tm, tn), lambda i, j: (i, j)),
        scratch_shapes=[pltpu.VMEM((tm, K), BF16)],
        compiler_params=_cparams(("parallel", "arbitrary")),
        name=name,
    )(x, g.reshape(1, K), wa_all, wb_all)


def _rep_lanes(x, n):
    return jnp.concatenate([x] * n, axis=1)


def _softmax_step(s, m_ref, l_ref, acc_ref, v):
    n_rep = s.shape[1] // LANES
    m_prev = m_ref[...]
    m_new = jnp.maximum(m_prev, jnp.max(s, axis=-1, keepdims=True))
    alpha = jnp.exp2(m_prev - m_new)
    p = jnp.exp2(s - _rep_lanes(m_new, n_rep))
    p_sum = p[:, :LANES]
    for c in range(1, n_rep):
        p_sum = p_sum + p[:, c * LANES:(c + 1) * LANES]
    l_ref[...] = alpha * l_ref[...] + p_sum
    acc_ref[...] = alpha * acc_ref[...] + jnp.dot(p.astype(BF16), v, preferred_element_type=F32)
    m_ref[...] = m_new


def _softmax_step_bounded(s, shift, l_ref, acc_ref, v):
    n_rep = s.shape[1] // LANES
    p = jnp.exp2(s - shift)
    p_sum = p[:, :LANES]
    for c in range(1, n_rep):
        p_sum = p_sum + p[:, c * LANES:(c + 1) * LANES]
    l_ref[...] += p_sum
    acc_ref[...] += jnp.dot(p.astype(BF16), v, preferred_element_type=F32)


def _qk_norm_score_bound(dim, scale, q_gain, k_gain):
    return dim * scale * jnp.max(jnp.abs(q_gain)) * jnp.max(jnp.abs(k_gain)) * BOUND_MARGIN


def _softmax_result(l_ref, acc_ref):
    return acc_ref[...] / jnp.sum(l_ref[...], axis=-1, keepdims=True)


def _sweep_far_blocks(n_far, scores, update):
    def pair(p, carry):
        s_a = scores(2 * p)
        s_b = scores(2 * p + 1)
        update(2 * p, s_a)
        update(2 * p + 1, s_b)
        return carry
    lax.fori_loop(0, n_far // 2, pair, 0)

    @pl.when(n_far % 2 == 1)
    def _():
        update(n_far - 1, scores(n_far - 1))


def _softmax_init(m_ref, l_ref, acc_ref):
    m_ref[...] = jnp.full(m_ref.shape, NEG_BIG, F32)
    l_ref[...] = jnp.zeros(l_ref.shape, F32)
    acc_ref[...] = jnp.zeros(acc_ref.shape, F32)


def _t5_tiles_body(rb_ref, o_ref, max_ref, *, tq):
    h = pl.program_id(0)
    row = lax.broadcasted_iota(I32, (tq, tq), 0)
    col = lax.broadcasted_iota(I32, (tq, tq), 1)
    max_exact = REL_BUCKETS // 2
    far = rb_ref[(REL_BUCKETS - 1) * DIFF_HEADS + h]
    bias_max = jnp.float32(0.0)
    for b in range(REL_BUCKETS - 1):
        bias_max = jnp.maximum(bias_max, (rb_ref[b * DIFF_HEADS + h] - far) * LOG2_E)
    max_ref[...] = jnp.full(max_ref.shape, bias_max, F32)
    for which in range(2):
        d = row - col + which * tq
        n = jnp.maximum(d, 0)
        nf = jnp.maximum(n, 1).astype(F32)
        large = max_exact + (jnp.log(nf / max_exact) / math.log(REL_MAX_DIST / max_exact)
                             * (REL_BUCKETS - max_exact)).astype(I32)
        large = jnp.minimum(large, REL_BUCKETS - 1)
        bucket = jnp.where(n < max_exact, n, large)
        val = jnp.zeros((tq, tq), F32)
        for b in range(REL_BUCKETS - 1):
            val = jnp.where(bucket == b, (rb_ref[b * DIFF_HEADS + h] - far) * LOG2_E, val)
        o_ref[0, which] = jnp.where(d >= 0, val, NEG_BIG)


def _t5_bias_tiles(rel_bias, tq):
    assert tq >= REL_MAX_DIST
    return pl.pallas_call(
        functools.partial(_t5_tiles_body, tq=tq),
        out_shape=(jax.ShapeDtypeStruct((DIFF_HEADS, 2, tq, tq), F32),
                   jax.ShapeDtypeStruct((DIFF_HEADS, 8, LANES), F32)),
        grid=(DIFF_HEADS,),
        in_specs=[pl.BlockSpec(memory_space=pltpu.SMEM)],
        out_specs=(pl.BlockSpec((1, 2, tq, tq), lambda h: (h, 0, 0, 0)),
                   pl.BlockSpec((1, 8, LANES), lambda h: (h, 0, 0))),
        compiler_params=_cparams(("parallel",)),
        name="t5_bias_tiles",
    )(rel_bias.reshape(-1))


def _diff_attn_body(lam_ref, qg_ref, kg_ref, sg_ref, bias_ref, bias_max_ref, q_ref, k_ref, v_ref, o_ref,
                    kn_s, m1, l1, a1, m2, l2, a2, *, tq, seq):
    qi = pl.program_id(2)
    lo = lax.broadcasted_iota(I32, (1, HEAD_W), 1) < DIFF_QK_DIM
    in_lo_r = lax.broadcasted_iota(I32, (HEAD_W, HEAD_W), 0) < DIFF_QK_DIM
    in_lo_c = lax.broadcasted_iota(I32, (HEAD_W, HEAD_W), 1) < DIFF_QK_DIM
    same_half = jnp.where(in_lo_r == in_lo_c, 1.0, 0.0).astype(BF16)
    same_half2 = jnp.concatenate([same_half, same_half], axis=0)

    def half_rms(x, g):
        x2 = x * x
        hi = x2.astype(BF16)
        lo2 = (x2 - hi.astype(F32)).astype(BF16)
        ss = jnp.dot(jnp.concatenate([hi, lo2], axis=1), same_half2, preferred_element_type=F32)
        return x * lax.rsqrt(ss * (1.0 / DIFF_QK_DIM) + EPS) * g

    @pl.when(qi == 0)
    def _():
        def norm_chunk(c, carry):
            rows = pl.ds(pl.multiple_of(c * tq, tq), tq)
            kn_s[rows, :] = half_rms(k_ref[rows, :].astype(F32), kg_ref[...]).astype(BF16)
            return carry
        lax.fori_loop(0, seq // tq, norm_chunk, 0, unroll=True)

    qn = half_rms(q_ref[...].astype(F32), qg_ref[...]) * (DIFF_QK_DIM ** -0.5 * LOG2_E)
    q1 = jnp.where(lo, qn, 0.0).astype(BF16)
    q2 = jnp.where(lo, 0.0, qn).astype(BF16)

    _softmax_init(m1, l1, a1)
    _softmax_init(m2, l2, a2)

    def block_rows(j):
        return pl.ds(pl.multiple_of(j * tq, tq), tq)

    def scores(j):
        kb = kn_s[block_rows(j), :]
        return (lax.dot_general(q1, kb, _NT, preferred_element_type=F32),
                lax.dot_general(q2, kb, _NT, preferred_element_type=F32))

    def sweep_all_blocks(softmax_step):
        def update(j, s, bias=None):
            s1, s2 = s if bias is None else (s[0] + bias, s[1] + bias)
            vb = v_ref[block_rows(j), :]
            softmax_step(s1, m1, l1, a1, vb)
            softmax_step(s2, m2, l2, a2, vb)

        _sweep_far_blocks(jnp.maximum(qi - 1, 0), scores, update)

        @pl.when(qi >= 1)
        def _():
            s_prev = scores(qi - 1)
            s_diag = scores(qi)
            update(qi - 1, s_prev, bias_ref[0, 1])
            update(qi, s_diag, bias_ref[0, 0])

        @pl.when(qi == 0)
        def _():
            update(0, scores(0), bias_ref[0, 0])

    bound = (_qk_norm_score_bound(DIFF_QK_DIM, DIFF_QK_DIM ** -0.5 * LOG2_E, qg_ref[...], kg_ref[...])
             + jnp.max(bias_max_ref[...]))
    bounded = 2.0 * bound <= BOUNDED_SOFTMAX_RANGE

    @pl.when(bounded)
    def _():
        sweep_all_blocks(lambda s, m, l, acc, v: _softmax_step_bounded(s, bound, l, acc, v))

    @pl.when(jnp.logical_not(bounded))
    def _():
        sweep_all_blocks(_softmax_step)

    lp = lam_ref[...]
    lam_init = lp[4:5, 0:1]
    lam = (jnp.exp(jnp.sum(lp[0:1] * lp[1:2], axis=-1, keepdims=True))
           - jnp.exp(jnp.sum(lp[2:3] * lp[3:4], axis=-1, keepdims=True)) + lam_init)
    o = _softmax_result(l1, a1) - lam * _softmax_result(l2, a2)
    ms = jnp.mean(o * o, axis=-1, keepdims=True)
    o = o * lax.rsqrt(ms + EPS) * sg_ref[...] * (1.0 - lam_init)
    o_ref[...] = o.astype(BF16)


def _diff_attention(proj, lam_tab, q_gain, k_gain, sub_gain, bias_tiles, bias_max, batch, seq, tq):
    T = proj.shape[0]
    nq = seq // tq
    H = DIFF_HEADS
    small = lambda b, h, i: (0, 0)
    return pl.pallas_call(
        functools.partial(_diff_attn_body, tq=tq, seq=seq),
        out_shape=jax.ShapeDtypeStruct((T, BRANCH_WIDTH), BF16),
        grid=(batch, H, nq),
        in_specs=[pl.BlockSpec((8, DIFF_QK_DIM), small),
                  pl.BlockSpec((1, HEAD_W), small),
                  pl.BlockSpec((1, HEAD_W), small),
                  pl.BlockSpec((1, HEAD_W), small),
                  pl.BlockSpec((1, 2, tq, tq), lambda b, h, i: (h, 0, 0, 0)),
                  pl.BlockSpec((1, 8, LANES), lambda b, h, i: (h, 0, 0)),
                  pl.BlockSpec((tq, HEAD_W), lambda b, h, i: (b * nq + i, h)),
                  pl.BlockSpec((seq, HEAD_W), lambda b, h, i: (b, H + h)),
                  pl.BlockSpec((seq, HEAD_W), lambda b, h, i: (b, 2 * H + h))],
        out_specs=pl.BlockSpec((tq, HEAD_W), lambda b, h, i: (b * nq + i, h)),
        scratch_shapes=[pltpu.VMEM((seq, HEAD_W), BF16),
                        pltpu.VMEM((tq, LANES), F32), pltpu.VMEM((tq, LANES), F32), pltpu.VMEM((tq, HEAD_W), F32),
                        pltpu.VMEM((tq, LANES), F32), pltpu.VMEM((tq, LANES), F32), pltpu.VMEM((tq, HEAD_W), F32)],
        compiler_params=_cparams(("parallel", "parallel", "arbitrary")),
        name="diff_attn",
    )(lam_tab, q_gain, k_gain, sub_gain, bias_tiles, bias_max, proj, proj, proj)


def _sb_attn_body(q_ref, k_ref, v_ref, o_ref, *scratch, sb, n_sub):
    qi = pl.program_id(2)
    row = lax.broadcasted_iota(I32, (sb, sb), 0)
    col = lax.broadcasted_iota(I32, (sb, sb), 1)
    tri = jnp.where(row >= col, 1.0, 0.0).astype(BF16)
    tri2 = jnp.concatenate([tri, tri], axis=0)
    strict = col < row
    n_rep = sb // LANES

    c_refs, acc_refs = scratch[:n_sub], scratch[n_sub:]
    for ref in scratch:
        ref[...] = jnp.zeros(ref.shape, F32)

    def sweep(it, masked):
        subs = range(n_sub)
        blks = [qi * n_sub + a - it for a in subs]
        rows = [pl.ds(pl.multiple_of(jnp.maximum(b, 0) * sb, sb), sb) for b in blks]
        zs = [lax.dot_general(q_ref[a * sb:(a + 1) * sb, :], k_ref[rows[a], :], _NT,
                              preferred_element_type=F32) * (SB_HEAD_DIM ** -0.5 * LOG2_E) for a in subs]
        log_fails, cums = [], []
        for a in subs:
            log_fail = -(jnp.maximum(zs[a], 0.0) + jnp.log2(1.0 + jnp.exp2(-jnp.abs(zs[a]))))
            if masked:
                log_fail = jnp.where(strict, log_fail, 0.0)
            hi = log_fail.astype(BF16)
            lo = (log_fail - hi.astype(F32)).astype(BF16)
            cums.append(jnp.dot(jnp.concatenate([hi, lo], axis=1), tri2, preferred_element_type=F32))
            log_fails.append(log_fail)
        c_max = jnp.float32(-jnp.inf)
        for a in subs:
            c = c_refs[a][...]
            dead = jnp.where(blks[a] >= 0, 0.0, NEG_BIG)
            w = jnp.exp2(zs[a] + cums[a] + _rep_lanes(c + dead, n_rep))
            if masked:
                w = jnp.where(strict, w, 0.0)
            acc_refs[a][...] += jnp.dot(w.astype(BF16), v_ref[rows[a], :], preferred_element_type=F32)
            c_new = c + jnp.sum(log_fails[a], axis=-1, keepdims=True)
            c_refs[a][...] = c_new
            c_max = jnp.maximum(c_max, jnp.where(blks[a] >= 1, jnp.max(c_new), -jnp.inf))
        return c_max

    c_max = sweep(0, True)

    def cond(carry):
        return carry[1] > SB_SKIP_LOG

    def body(carry):
        return carry[0] + 1, sweep(carry[0], False)

    lax.while_loop(cond, body, (jnp.int32(1), c_max))
    for a in range(n_sub):
        o_ref[a * sb:(a + 1) * sb, :] = acc_refs[a][...].astype(BF16)


def _sb_attention(proj, batch, seq, sb):
    T = proj.shape[0]
    n_sub = _tile(seq // sb, 4)
    tq = sb * n_sub
    nq = seq // tq
    H = SB_HEADS
    base = 3 * DIFF_HEADS
    return pl.pallas_call(
        functools.partial(_sb_attn_body, sb=sb, n_sub=n_sub),
        out_shape=jax.ShapeDtypeStruct((T, BRANCH_WIDTH), BF16),
        grid=(batch, H, nq),
        in_specs=[pl.BlockSpec((tq, HEAD_W), lambda b, h, i: (b * nq + i, base + h)),
                  pl.BlockSpec((seq, HEAD_W), lambda b, h, i: (b, base + H + h)),
                  pl.BlockSpec((seq, HEAD_W), lambda b, h, i: (b, base + 2 * H + h))],
        out_specs=pl.BlockSpec((tq, HEAD_W), lambda b, h, i: (b * nq + i, h)),
        scratch_shapes=([pltpu.VMEM((sb, LANES), F32)] * n_sub + [pltpu.VMEM((sb, HEAD_W), F32)] * n_sub),
        compiler_params=_cparams(("parallel", "parallel", "arbitrary")),
        name="sb_attn",
    )(proj, proj, proj)


def _mla_prep_body(pm_ref, qa_ref, kva_ref, wuq_ref, wukv_ref, qg_ref, kg_ref,
                   cos_ref, sina_ref, sinb_ref, q_ref, k_ref, v_ref):
    x = pm_ref[...].astype(F32)

    def rms(t, g):
        return t * lax.rsqrt(jnp.mean(t * t, axis=-1, keepdims=True) + EPS) * g

    cq = rms(x[:, :MLA_Q_RANK], qa_ref[...]).astype(BF16)
    ckv = rms(x[:, MLA_Q_RANK:MLA_Q_RANK + MLA_KV_RANK], kva_ref[...]).astype(BF16)
    kr = x[:, MLA_Q_RANK + MLA_KV_RANK:MLA_Q_RANK + MLA_KV_RANK + LANES]
    q = jnp.dot(cq, wuq_ref[...], preferred_element_type=F32)
    kv = jnp.dot(ckv, wukv_ref[...], preferred_element_type=F32)
    cos, sina, sinb = cos_ref[...], sina_ref[...], sinb_ref[...]
    half = MLA_ROPE_DIM // 2

    def rope(pe):
        return pe * cos + pltpu.roll(pe, LANES - half, 1) * sina + pltpu.roll(pe, half, 1) * sinb

    ones2 = jnp.ones((2 * LANES, LANES), BF16)

    def lane_sum(t):
        hi = t.astype(BF16)
        lo = (t - hi.astype(F32)).astype(BF16)
        return jnp.dot(jnp.concatenate([hi, lo], axis=1), ones2, preferred_element_type=F32)

    k_pe = rope(kr)
    k_pe_sq = k_pe * k_pe
    qk_dim = MLA_NOPE_DIM + MLA_ROPE_DIM
    qg, kg = qg_ref[...], kg_ref[...]
    for h in range(MLA_HEADS):
        o = h * MLA_QK_PAD
        q_n = q[:, o:o + MLA_NOPE_DIM]
        q_pe = rope(q[:, o + MLA_NOPE_DIM:o + MLA_QK_PAD])
        ss = lane_sum(q_n * q_n + q_pe * q_pe)
        r = lax.rsqrt(ss * (1.0 / qk_dim) + EPS) * (qk_dim ** -0.5 * LOG2_E)
        q_ref[:, o:o + MLA_NOPE_DIM] = (q_n * r * qg[:, :MLA_NOPE_DIM]).astype(BF16)
        q_ref[:, o + MLA_NOPE_DIM:o + MLA_QK_PAD] = (q_pe * r * qg[:, MLA_NOPE_DIM:]).astype(BF16)
        k_n = kv[:, o:o + MLA_NOPE_DIM]
        ssk = lane_sum(k_n * k_n + k_pe_sq)
        rk = lax.rsqrt(ssk * (1.0 / qk_dim) + EPS)
        k_ref[:, o:o + MLA_NOPE_DIM] = (k_n * rk * kg[:, :MLA_NOPE_DIM]).astype(BF16)
        k_ref[:, o + MLA_NOPE_DIM:o + MLA_QK_PAD] = (k_pe * rk * kg[:, MLA_NOPE_DIM:]).astype(BF16)
        v_ref[:, h * MLA_V_DIM:(h + 1) * MLA_V_DIM] = kv[:, o + MLA_NOPE_DIM:o + MLA_QK_PAD].astype(BF16)


def _mla_prep(proj, mla_blk, qa_g, kva_g, w_uq, w_ukv, q_g, k_g, cos_t, sina_t, sinb_t, seq):
    T = proj.shape[0]
    tm = _tile(seq, 512)
    ns = seq // tm
    W = MLA_HEADS * MLA_QK_PAD
    const = lambda i: (0, 0)
    pos = lambda i: (i % ns, 0)
    return pl.pallas_call(
        _mla_prep_body,
        out_shape=(jax.ShapeDtypeStruct((T, W), BF16), jax.ShapeDtypeStruct((T, W), BF16),
                   jax.ShapeDtypeStruct((T, MLA_HEADS * MLA_V_DIM), BF16)),
        grid=(T // tm,),
        in_specs=[pl.BlockSpec((tm, MLA_IN_W), lambda i: (i, mla_blk)),
                  pl.BlockSpec((1, MLA_Q_RANK), const),
                  pl.BlockSpec((1, MLA_KV_RANK), const),
                  pl.BlockSpec((MLA_Q_RANK, W), const),
                  pl.BlockSpec((MLA_KV_RANK, W), const),
                  pl.BlockSpec((1, MLA_QK_PAD), const),
                  pl.BlockSpec((1, MLA_QK_PAD), const),
                  pl.BlockSpec((tm, LANES), pos),
                  pl.BlockSpec((tm, LANES), pos),
                  pl.BlockSpec((tm, LANES), pos)],
        out_specs=(pl.BlockSpec((tm, W), lambda i: (i, 0)),
                   pl.BlockSpec((tm, W), lambda i: (i, 0)),
                   pl.BlockSpec((tm, MLA_HEADS * MLA_V_DIM), lambda i: (i, 0))),
        compiler_params=_cparams(("parallel",)),
        name="mla_prep",
    )(proj, qa_g, kva_g, w_uq, w_ukv, q_g, k_g, cos_t, sina_t, sinb_t)


def _mla_attn_body(qg_ref, kg_ref, q_ref, k_ref, v_ref, o_ref, m, l, acc, *, tq):
    qi = pl.program_id(2)
    q = q_ref[...]
    _softmax_init(m, l, acc)

    def block_rows(j):
        return pl.ds(pl.multiple_of(j * tq, tq), tq)

    def scores(j):
        return lax.dot_general(q, k_ref[block_rows(j), :], _NT, preferred_element_type=F32)

    causal = (lax.broadcasted_iota(I32, (tq, tq), 1) <= lax.broadcasted_iota(I32, (tq, tq), 0))

    def sweep_all_blocks(softmax_step):
        def update(j, s):
            softmax_step(s, m, l, acc, v_ref[block_rows(j), :])

        _sweep_far_blocks(jnp.maximum(qi - 1, 0), scores, update)

        @pl.when(qi >= 1)
        def _():
            s_prev = scores(qi - 1)
            s_diag = scores(qi)
            update(qi - 1, s_prev)
            update(qi, jnp.where(causal, s_diag, NEG_BIG))

        @pl.when(qi == 0)
        def _():
            update(0, jnp.where(causal, scores(0), NEG_BIG))

    qk_dim = MLA_NOPE_DIM + MLA_ROPE_DIM
    bound = _qk_norm_score_bound(qk_dim, qk_dim ** -0.5 * LOG2_E, qg_ref[...], kg_ref[...])
    bounded = 2.0 * bound <= BOUNDED_SOFTMAX_RANGE

    @pl.when(bounded)
    def _():
        sweep_all_blocks(lambda s, m_, l_, acc_, v: _softmax_step_bounded(s, bound, l_, acc_, v))

    @pl.when(jnp.logical_not(bounded))
    def _():
        sweep_all_blocks(_softmax_step)
    o_ref[...] = _softmax_result(l, acc).astype(BF16)


def _mla_attention(q, k, v, q_gain, k_gain, batch, seq, tq):
    T = q.shape[0]
    nq = seq // tq
    gain_spec = pl.BlockSpec((1, MLA_QK_PAD), lambda b, h, i: (0, 0))
    return pl.pallas_call(
        functools.partial(_mla_attn_body, tq=tq),
        out_shape=jax.ShapeDtypeStruct((T, BRANCH_WIDTH), BF16),
        grid=(batch, MLA_HEADS, nq),
        in_specs=[gain_spec, gain_spec,
                  pl.BlockSpec((tq, MLA_QK_PAD), lambda b, h, i: (b * nq + i, h)),
                  pl.BlockSpec((seq, MLA_QK_PAD), lambda b, h, i: (b, h)),
                  pl.BlockSpec((seq, MLA_V_DIM), lambda b, h, i: (b, h))],
        out_specs=pl.BlockSpec((tq, MLA_V_DIM), lambda b, h, i: (b * nq + i, h)),
        scratch_shapes=[pltpu.VMEM((tq, LANES), F32), pltpu.VMEM((tq, LANES), F32),
                        pltpu.VMEM((tq, MLA_V_DIM), F32)],
        compiler_params=_cparams(("parallel", "parallel", "arbitrary")),
        name="mla_attn",
    )(q_gain, k_gain, q, k, v)


def _merge_body(oa_ref, ob_ref, oc_ref, wb_ref, ga_ref, gb_ref, gc_ref, bg_ref, o_ref):
    acc = None
    for b, (o_r, g_r) in enumerate(((oa_ref, ga_ref), (ob_ref, gb_ref), (oc_ref, gc_ref))):
        gate = jax.nn.sigmoid(g_r[...].astype(F32) + bg_ref[b:b + 1, :])
        t = gate * jnp.dot(o_r[...], wb_ref[0, b], preferred_element_type=F32)
        acc = t if acc is None else acc + t
    o_ref[...] = acc.astype(BF16)


def _branch_merge(o_a, o_b, o_c, w_branch_all, layer, proj, b_gate, d_model):
    T = o_a.shape[0]
    tm, tn = _tile(T, 1024), _tile(d_model, 512)
    g0 = (QKV_W + MLA_IN_W) // tn
    nj = d_model // tn
    o_spec = pl.BlockSpec((tm, BRANCH_WIDTH), lambda i, j: (i, 0))
    gate_spec = lambda b: pl.BlockSpec((tm, tn), lambda i, j: (i, g0 + b * nj + j))
    return pl.pallas_call(
        _merge_body,
        out_shape=jax.ShapeDtypeStruct((T, d_model), BF16),
        grid=(T // tm, nj),
        in_specs=[o_spec, o_spec, o_spec,
                  pl.BlockSpec((1, N_BRANCH, BRANCH_WIDTH, tn), lambda i, j: (layer, 0, 0, j)),
                  gate_spec(0), gate_spec(1), gate_spec(2),
                  pl.BlockSpec((N_BRANCH, tn), lambda i, j: (0, j))],
        out_specs=pl.BlockSpec((tm, tn), lambda i, j: (i, j)),
        compiler_params=_cparams(("parallel", "arbitrary")),
        name="branch_merge",
    )(o_a, o_b, o_c, w_branch_all, proj, proj, proj, b_gate)


def _out_proj_body(a_ref, w_ref, r_ref, o_ref):
    o_ref[...] = r_ref[...] + jnp.dot(a_ref[...], w_ref[0], preferred_element_type=F32)


def _out_proj(merged, w_out_all, layer, x):
    T, K = merged.shape
    N = w_out_all.shape[2]
    tm, tn = _tile(T, 1024), _tile(N, 1024)
    return pl.pallas_call(
        _out_proj_body,
        out_shape=jax.ShapeDtypeStruct((T, N), F32),
        grid=(T // tm, N // tn),
        in_specs=[pl.BlockSpec((tm, K), lambda i, j: (i, 0)),
                  pl.BlockSpec((1, K, tn), lambda i, j: (layer, 0, j)),
                  pl.BlockSpec((tm, tn), lambda i, j: (i, j))],
        out_specs=pl.BlockSpec((tm, tn), lambda i, j: (i, j)),
        compiler_params=_cparams(("parallel", "arbitrary")),
        name="out_proj",
    )(merged, w_out_all, x)


def _route_body(x_ref, g_ref, wr_ref, br_ref, comb_ref, meta_ref, cnt_ref, carry_s, *, tm):
    i = pl.program_id(0)

    @pl.when(i == 0)
    def _():
        carry_s[...] = jnp.zeros(carry_s.shape, F32)

    x = x_ref[...]
    h = x * lax.rsqrt(jnp.mean(x * x, axis=-1, keepdims=True) + EPS) * g_ref[...]
    logits = jnp.dot(h, wr_ref[...], preferred_element_type=F32,
                     precision=lax.Precision.HIGHEST) + br_ref[...]
    lane = lax.broadcasted_iota(I32, (tm, LANES), 1)
    big = jnp.int32(LANES)

    gl = jnp.where(lane < N_GROUPS, logits, -jnp.inf)
    g_max = jnp.max(gl, axis=-1, keepdims=True)
    g_idx = jnp.min(jnp.where(gl == g_max, lane, big), axis=-1, keepdims=True)
    g_val = 1.0 / jnp.sum(jnp.exp(gl - g_max), axis=-1, keepdims=True)

    e_lo = N_GROUPS + g_idx * EXPERTS_PER_GROUP
    el = jnp.where(jnp.logical_and(lane >= e_lo, lane < e_lo + EXPERTS_PER_GROUP), logits, -jnp.inf)
    e1 = jnp.max(el, axis=-1, keepdims=True)
    i1 = jnp.min(jnp.where(el == e1, lane, big), axis=-1, keepdims=True)
    el2 = jnp.where(lane == i1, -jnp.inf, el)
    e2 = jnp.max(el2, axis=-1, keepdims=True)
    i2 = jnp.min(jnp.where(el2 == e2, lane, big), axis=-1, keepdims=True)
    t = jnp.exp(e2 - e1)
    w1 = g_val / (1.0 + t)
    w2 = g_val * t / (1.0 + t)
    comb_ref[...] = jnp.where(lane == i1, w1, 0.0) + jnp.where(lane == i2, w2, 0.0)

    onehot = jnp.where(lane == g_idx, 1.0, 0.0)
    r = lax.broadcasted_iota(I32, (tm, tm), 0)
    c = lax.broadcasted_iota(I32, (tm, tm), 1)
    before = jnp.where(c < r, 1.0, 0.0).astype(BF16)
    prefix = jnp.dot(before, onehot.astype(BF16), preferred_element_type=F32) + carry_s[...]
    rank = jnp.sum(jnp.where(lane == g_idx, prefix, 0.0), axis=-1, keepdims=True)
    carry_s[...] += jnp.sum(onehot, axis=0, keepdims=True)
    meta_ref[...] = jnp.where(lane == 0, g_idx, jnp.where(lane == 1, rank.astype(I32), 0))
    cnt_ref[...] = jnp.broadcast_to(carry_s[...], cnt_ref.shape).astype(I32)


def _moe_route(x, g, w_r, b_r):
    T, D = x.shape
    tm = _tile(T, 512)
    return pl.pallas_call(
        functools.partial(_route_body, tm=tm),
        out_shape=(jax.ShapeDtypeStruct((T, LANES), F32),
                   jax.ShapeDtypeStruct((T, LANES), I32), jax.ShapeDtypeStruct((8, LANES), I32)),
        grid=(T // tm,),
        in_specs=[pl.BlockSpec((tm, D), lambda i: (i, 0)),
                  pl.BlockSpec((1, D), lambda i: (0, 0)),
                  pl.BlockSpec((D, LANES), lambda i: (0, 0)),
                  pl.BlockSpec((1, LANES), lambda i: (0, 0))],
        out_specs=(pl.BlockSpec((tm, LANES), lambda i: (i, 0)),
                   pl.BlockSpec((tm, LANES), lambda i: (i, 0)),
                   pl.BlockSpec((8, LANES), lambda i: (0, 0))),
        scratch_shapes=[pltpu.VMEM((1, LANES), F32)],
        compiler_params=_cparams(("arbitrary",)),
        name="moe_route",
    )(x, g.reshape(1, D), w_r, b_r)


def _start_and_wait_rows(row_copy, n_rows):
    def issue(r, carry):
        row_copy(r).start()
        return carry
    lax.fori_loop(0, n_rows, issue, 0, unroll=ROW_DMA_UNROLL)

    def drain(r, carry):
        row_copy(r).wait()
        return carry
    lax.fori_loop(0, n_rows, drain, 0, unroll=ROW_DMA_UNROLL)


def _dispatch_body(dest_ref, x_ref, g_ref, comb_ref, xs_in, xs_ref, row_buf, sem, *, tm):
    del xs_in
    d_model = x_ref.shape[1]
    x = x_ref[...]
    row_buf[:, :d_model] = x * lax.rsqrt(jnp.mean(x * x, axis=-1, keepdims=True) + EPS) * g_ref[...]
    row_buf[:, d_model:] = comb_ref[...]
    base = pl.program_id(0) * tm

    def row_copy(r):
        return pltpu.make_async_copy(row_buf.at[pl.ds(r, 1)], xs_ref.at[pl.ds(dest_ref[base + r], 1)], sem.at[0])
    _start_and_wait_rows(row_copy, tm)


def _moe_dispatch(dest, x, g, comb, t_pad):
    T, D = x.shape
    W = D + LANES
    tm = _tile(T, ROW_DMA_TILE)
    return pl.pallas_call(
        functools.partial(_dispatch_body, tm=tm),
        out_shape=jax.ShapeDtypeStruct((t_pad, W), F32),
        grid_spec=pltpu.PrefetchScalarGridSpec(
            num_scalar_prefetch=1, grid=(T // tm,),
            in_specs=[pl.BlockSpec((tm, D), lambda i, d: (i, 0)),
                      pl.BlockSpec((1, D), lambda i, d: (0, 0)),
                      pl.BlockSpec((tm, LANES), lambda i, d: (i, 0)),
                      pl.BlockSpec(memory_space=pl.ANY)],
            out_specs=pl.BlockSpec(memory_space=pl.ANY),
            scratch_shapes=[pltpu.VMEM((tm, W), F32), pltpu.SemaphoreType.DMA((1,))]),
        input_output_aliases={4: 0},
        compiler_params=_cparams(("arbitrary",)),
        name="moe_dispatch",
    )(dest, x, g.reshape(1, D), comb, jnp.zeros((t_pad, W), F32))


def _gather_rows_into(idx_ref, table_ref, dst_ref, sem, n_rows):
    base = pl.program_id(0) * n_rows

    def row_copy(r):
        return pltpu.make_async_copy(table_ref.at[pl.ds(idx_ref[base + r], 1)], dst_ref.at[pl.ds(r, 1)], sem.at[0])
    _start_and_wait_rows(row_copy, n_rows)


def _experts_body(tg_ref, nt_ref, xs_ref, wgu_ref, wd_ref, y_ref, *, d_expert):
    i, e = pl.program_id(0), pl.program_id(1)
    valid = i < nt_ref[0]
    d_model = y_ref.shape[1]

    @pl.when(valid)
    def _():
        x = xs_ref[:, :d_model].astype(BF16)
        gu = jnp.dot(x, wgu_ref[0], preferred_element_type=F32)
        comb = xs_ref[:, d_model:]
        lane = lax.broadcasted_iota(I32, comb.shape, 1)
        sel = N_GROUPS + tg_ref[i] * EXPERTS_PER_GROUP + e
        c = jnp.sum(jnp.where(lane == sel, comb, 0.0), axis=-1, keepdims=True)
        a = (jax.nn.silu(gu[:, :d_expert]) * gu[:, d_expert:] * c).astype(BF16)
        y = jnp.dot(a, wd_ref[0], preferred_element_type=F32)

        @pl.when(e == 0)
        def _():
            y_ref[...] = y

        @pl.when(e > 0)
        def _():
            y_ref[...] += y

    @pl.when(jnp.logical_and(jnp.logical_not(valid), e == 0))
    def _():
        y_ref[...] = jnp.zeros(y_ref.shape, F32)


def _moe_experts(tile_group, n_tiles, xs, w_gu, w_down, layer, tg):
    t_pad = xs.shape[0]
    d_expert, D = w_down.shape[1:]
    expert = lambda i, e, tgr, ntr: (layer * N_EXPERTS + tgr[i] * EXPERTS_PER_GROUP + e, 0, 0)
    rows = lambda i, e, tgr, ntr: (i, 0)
    return pl.pallas_call(
        functools.partial(_experts_body, d_expert=d_expert),
        out_shape=jax.ShapeDtypeStruct((t_pad, D), F32),
        grid_spec=pltpu.PrefetchScalarGridSpec(
            num_scalar_prefetch=2, grid=(t_pad // tg, EXPERTS_PER_GROUP),
            in_specs=[pl.BlockSpec((tg, D + LANES), rows),
                      pl.BlockSpec((1, D, 2 * d_expert), expert),
                      pl.BlockSpec((1, d_expert, D), expert)],
            out_specs=pl.BlockSpec((tg, D), rows)),
        compiler_params=_cparams(("parallel", "arbitrary")),
        name="moe_experts",
    )(tile_group, n_tiles, xs, w_gu, w_down)


def _combine_body(dest_ref, x_ref, ys_ref, o_ref, buf, sem, *, tm):
    _gather_rows_into(dest_ref, ys_ref, buf, sem, tm)
    o_ref[...] = x_ref[...] + buf[...]


def _moe_combine(dest, x, ys):
    T, D = x.shape
    tm = _tile(T, ROW_DMA_TILE)
    return pl.pallas_call(
        functools.partial(_combine_body, tm=tm),
        out_shape=jax.ShapeDtypeStruct((T, D), F32),
        grid_spec=pltpu.PrefetchScalarGridSpec(
            num_scalar_prefetch=1, grid=(T // tm,),
            in_specs=[pl.BlockSpec((tm, D), lambda i, d: (i, 0)),
                      pl.BlockSpec(memory_space=pl.ANY)],
            out_specs=pl.BlockSpec((tm, D), lambda i, d: (i, 0)),
            scratch_shapes=[pltpu.VMEM((tm, D), F32), pltpu.SemaphoreType.DMA((1,))]),
        compiler_params=_cparams(("arbitrary",)),
        name="moe_combine",
    )(dest, x, ys)


def _hier_moe(x, ln_g, w_r, b_r, w_gu, w_down, layer):
    T, D = x.shape
    tg = _tile(T, 512)
    comb, meta, counts = _moe_route(x, ln_g, w_r, b_r)
    cnt = counts[0, :N_GROUPS]
    seg_tiles = (cnt + tg - 1) // tg
    tile_end = jnp.cumsum(seg_tiles)
    seg_off = (tile_end - seg_tiles) * tg
    dest = seg_off[meta[:, 0]] + meta[:, 1]
    n_tiles_max = T // tg + N_GROUPS
    n_tiles = tile_end[-1]
    tile_ids = jnp.arange(n_tiles_max, dtype=I32)
    tile_group = jnp.searchsorted(tile_end, jnp.minimum(tile_ids, n_tiles - 1), side="right").astype(I32)
    dest = dest.astype(I32)
    xs = _moe_dispatch(dest, x, ln_g, comb, n_tiles_max * tg)
    ys = _moe_experts(tile_group, n_tiles.reshape(1).astype(I32), xs, w_gu, w_down, layer, tg)
    return _moe_combine(dest, x, ys)


def _layout_w_in(w_in):
    mla_w = MLA_Q_RANK + MLA_KV_RANK + MLA_ROPE_DIM
    w_a = jnp.pad(w_in[:, :, :QKV_W + mla_w].astype(BF16), ((0, 0), (0, 0), (0, MLA_IN_W - mla_w)))
    w_b = w_in[:, :, QKV_W + mla_w:].astype(BF16)
    return w_a, w_b


def _layout_w_uq(w_uq):
    w = w_uq.reshape(MLA_Q_RANK, MLA_HEADS, MLA_NOPE_DIM + MLA_ROPE_DIM)
    w = jnp.pad(w, ((0, 0), (0, 0), (0, MLA_QK_PAD - MLA_NOPE_DIM - MLA_ROPE_DIM)))
    return w.reshape(MLA_Q_RANK, MLA_HEADS * MLA_QK_PAD).astype(BF16)


def _pad_gain(g):
    return jnp.pad(g, (0, MLA_QK_PAD - g.shape[0])).reshape(1, MLA_QK_PAD)


def _rope_tables(seq):
    pos = jnp.arange(seq, dtype=F32)
    inv_freq = 1.0 / (ROPE_THETA ** (jnp.arange(0, MLA_ROPE_DIM, 2, dtype=F32) / MLA_ROPE_DIM))
    ang = pos[:, None] * inv_freq[None, :]
    cos, sin = jnp.cos(ang), jnp.sin(ang)
    z = jnp.zeros_like(cos)
    return (jnp.concatenate([cos, cos, z, z], axis=1),
            jnp.concatenate([-sin, z, z, z], axis=1),
            jnp.concatenate([z, sin, z, z], axis=1))


def kernel(x, rel_bias, ln1_g, w_in, diff_q_norm, diff_k_norm, diff_lambda, diff_subln_g, mla_q_a_norm, mla_w_uq, mla_kv_a_norm, mla_w_ukv, mla_q_norm, mla_k_norm, w_branch, b_gate, w_out, ln2_g, router_group_w, router_group_b, router_expert_w, router_expert_b, expert_w_gu, expert_w_down):
    B, S, D = x.shape
    depth = w_in.shape[0]
    T = B * S
    tq = _tile(S, 512)
    tq_sb = _tile(S, 256)
    cos_t, sina_t, sinb_t = _rope_tables(S)
    bias_tiles, bias_max = _t5_bias_tiles(rel_bias, tq)
    w_in_a, w_in_b = _layout_w_in(w_in)
    mla_blk = QKV_W // MLA_IN_W
    w_branch_all = w_branch.astype(BF16)
    w_out_all = w_out.astype(BF16)
    w_gu_all = expert_w_gu.astype(BF16).reshape((depth * N_EXPERTS,) + expert_w_gu.shape[2:])
    w_down_all = expert_w_down.astype(BF16).reshape((depth * N_EXPERTS,) + expert_w_down.shape[2:])
    xt = x.reshape(T, D)
    for l in range(depth):
        proj = _rms_matmul(xt, ln1_g[l], w_in_a, w_in_b, l, BF16, "rms_in_proj")

        lam_init = 0.8 - 0.6 * math.exp(-0.3 * l)
        lam_tab = jnp.concatenate([diff_lambda[l], jnp.full((4, DIFF_QK_DIM), lam_init, F32)], axis=0)
        o_a = _diff_attention(proj, lam_tab,
                              jnp.tile(diff_q_norm[l], 2).reshape(1, HEAD_W),
                              jnp.tile(diff_k_norm[l], 2).reshape(1, HEAD_W),
                              diff_subln_g[l].reshape(1, HEAD_W), bias_tiles, bias_max, B, S, tq)
        o_b = _sb_attention(proj, B, S, tq_sb)
        mla_qg, mla_kg = _pad_gain(mla_q_norm[l]), _pad_gain(mla_k_norm[l])
        q_c, k_c, v_c = _mla_prep(proj, mla_blk, mla_q_a_norm[l].reshape(1, -1), mla_kv_a_norm[l].reshape(1, -1),
                                  _layout_w_uq(mla_w_uq[l]), mla_w_ukv[l].astype(BF16),
                                  mla_qg, mla_kg, cos_t, sina_t, sinb_t, S)
        o_c = _mla_attention(q_c, k_c, v_c, mla_qg, mla_kg, B, S, tq)

        merged = _branch_merge(o_a, o_b, o_c, w_branch_all, l, proj, b_gate[l].reshape(N_BRANCH, D), D)
        xt = _out_proj(merged, w_out_all, l, xt)

        w_r = jnp.concatenate([router_group_w[l], router_expert_w[l],
                               jnp.zeros((D, LANES - N_GROUPS - N_EXPERTS), F32)], axis=1)
        b_r = jnp.concatenate([router_group_b[l], router_expert_b[l],
                               jnp.zeros((LANES - N_GROUPS - N_EXPERTS,), F32)]).reshape(1, LANES)
        xt = _hier_moe(xt, ln2_g[l], w_r, b_r, w_gu_all, w_down_all, l)
    return xt.reshape(B, S, D)
```

```python
import functools
import math

import jax
import jax.numpy as jnp
from jax import lax
from jax.experimental import pallas as pl
from jax.experimental.pallas import tpu as pltpu

F32 = jnp.float32
BF16 = jnp.bfloat16
I32 = jnp.int32

EPS = 1e-6
LANES = 128
HEAD_W = 128
DIFF_HEADS = 8
DIFF_QK_DIM = 64
SB_HEADS = 8
SB_HEAD_DIM = 128
MLA_HEADS = 8
MLA_Q_RANK = 512
MLA_KV_RANK = 256
MLA_NOPE_DIM = 128
MLA_ROPE_DIM = 64
MLA_V_DIM = 128
MLA_QK_PAD = 256
ROPE_THETA = 10000.0
REL_BUCKETS = 32
REL_MAX_DIST = 128
N_GROUPS = 8
EXPERTS_PER_GROUP = 4
N_EXPERTS = N_GROUPS * EXPERTS_PER_GROUP
N_BRANCH = 3
BRANCH_WIDTH = 1024
QKV_W = 6 * BRANCH_WIDTH
MLA_IN_W = 1024
NEG_BIG = -1e30
LOG2_E = math.log2(math.e)
SB_SKIP_LOG = -100.0 * LOG2_E
VMEM_LIMIT = 56 * 1024 * 1024
BOUND_MARGIN = 1.02
BOUNDED_SOFTMAX_RANGE = 96.0
ROW_DMA_TILE = 512
ROW_DMA_UNROLL = 8

_NT = (((1,), (1,)), ((), ()))


def _cparams(sem, vmem=VMEM_LIMIT):
    return pltpu.CompilerParams(dimension_semantics=sem, vmem_limit_bytes=vmem)


def _tile(n, target):
    t = min(n, target)
    while n % t:
        t -= 1
    return t


def _round_up(n, m):
    return -(-n // m) * m


def _rms_mm_body(x_ref, g_ref, wa_ref, wb_ref, o_ref, h_ref, *, n_a):
    j = pl.program_id(1)

    @pl.when(j == 0)
    def _():
        x = x_ref[...]
        ms = jnp.mean(x * x, axis=-1, keepdims=True)
        h_ref[...] = (x * lax.rsqrt(ms + EPS) * g_ref[...]).astype(BF16)

    @pl.when(j < n_a)
    def _():
        o_ref[...] = jnp.dot(h_ref[...], wa_ref[0], preferred_element_type=F32).astype(o_ref.dtype)

    @pl.when(j >= n_a)
    def _():
        o_ref[...] = jnp.dot(h_ref[...], wb_ref[0], preferred_element_type=F32).astype(o_ref.dtype)


def _rms_matmul(x, g, wa_all, wb_all, layer, out_dtype, name):
    T, K = x.shape
    na_cols, nb_cols = wa_all.shape[2], wb_all.shape[2]
    tm, tn = _tile(T, 1024), _tile(math.gcd(na_cols, nb_cols), 1024)
    n_a, n_b = na_cols // tn, nb_cols // tn
    return pl.pallas_call(
        functools.partial(_rms_mm_body, n_a=n_a),
        out_shape=jax.ShapeDtypeStruct((T, na_cols + nb_cols), out_dtype),
        grid=(T // tm, n_a + n_b),
        in_specs=[pl.BlockSpec((tm, K), lambda i, j: (i, 0)),
                  pl.BlockSpec((1, K), lambda i, j: (0, 0)),
                  pl.BlockSpec((1, K, tn), lambda i, j: (layer, 0, jnp.minimum(j, n_a - 1))),
                  pl.BlockSpec((1, K, tn), lambda i, j: (layer, 0, jnp.where(j < n_a, n_b - 1, j - n_a)))],
        out_specs=pl.BlockSpec((tm, tn), lambda i, j: (i, j)),
        scratch_shapes=[pltpu.VMEM((tm, K), BF16)],
        compiler_params=_cparams(("parallel", "arbitrary")),
        name=name,
    )(x, g.reshape(1, K), wa_all, wb_all)


def _rep_lanes(x, n):
    return jnp.concatenate([x] * n, axis=1)


def _softmax_step(s, m_ref, l_ref, acc_ref, v):
    n_rep = s.shape[1] // LANES
    m_prev = m_ref[...]
    m_new = jnp.maximum(m_prev, jnp.max(s, axis=-1, keepdims=True))
    alpha = jnp.exp2(m_prev - m_new)
    p = jnp.exp2(s - _rep_lanes(m_new, n_rep))
    p_sum = p[:, :LANES]
    for c in range(1, n_rep):
        p_sum = p_sum + p[:, c * LANES:(c + 1) * LANES]
    l_ref[...] = alpha * l_ref[...] + p_sum
    acc_ref[...] = alpha * acc_ref[...] + jnp.dot(p.astype(BF16), v, preferred_element_type=F32)
    m_ref[...] = m_new


def _softmax_step_bounded(s, shift, l_ref, acc_ref, v):
    n_rep = s.shape[1] // LANES
    p = jnp.exp2(s - shift)
    p_sum = p[:, :LANES]
    for c in range(1, n_rep):
        p_sum = p_sum + p[:, c * LANES:(c + 1) * LANES]
    l_ref[...] += p_sum
    acc_ref[...] += jnp.dot(p.astype(BF16), v, preferred_element_type=F32)


def _qk_norm_score_bound(dim, scale, q_gain, k_gain):
    return dim * scale * jnp.max(jnp.abs(q_gain)) * jnp.max(jnp.abs(k_gain)) * BOUND_MARGIN


def _softmax_result(l_ref, acc_ref):
    return acc_ref[...] / jnp.sum(l_ref[...], axis=-1, keepdims=True)


def _sweep_far_blocks(n_far, scores, update):
    def pair(p, carry):
        s_a = scores(2 * p)
        s_b = scores(2 * p + 1)
        update(2 * p, s_a)
        update(2 * p + 1, s_b)
        return carry
    lax.fori_loop(0, n_far // 2, pair, 0)

    @pl.when(n_far % 2 == 1)
    def _():
        update(n_far - 1, scores(n_far - 1))


def _softmax_init(m_ref, l_ref, acc_ref):
    m_ref[...] = jnp.full(m_ref.shape, NEG_BIG, F32)
    l_ref[...] = jnp.zeros(l_ref.shape, F32)
    acc_ref[...] = jnp.zeros(acc_ref.shape, F32)


def _t5_tiles_body(rb_ref, o_ref, max_ref, *, tq):
    h = pl.program_id(0)
    row = lax.broadcasted_iota(I32, (tq, tq), 0)
    col = lax.broadcasted_iota(I32, (tq, tq), 1)
    max_exact = REL_BUCKETS // 2
    far = rb_ref[(REL_BUCKETS - 1) * DIFF_HEADS + h]
    bias_max = jnp.float32(0.0)
    for b in range(REL_BUCKETS - 1):
        bias_max = jnp.maximum(bias_max, (rb_ref[b * DIFF_HEADS + h] - far) * LOG2_E)
    max_ref[...] = jnp.full(max_ref.shape, bias_max, F32)
    for which in range(2):
        d = row - col + which * tq
        n = jnp.maximum(d, 0)
        nf = jnp.maximum(n, 1).astype(F32)
        large = max_exact + (jnp.log(nf / max_exact) / math.log(REL_MAX_DIST / max_exact)
                             * (REL_BUCKETS - max_exact)).astype(I32)
        large = jnp.minimum(large, REL_BUCKETS - 1)
        bucket = jnp.where(n < max_exact, n, large)
        val = jnp.zeros((tq, tq), F32)
        for b in range(REL_BUCKETS - 1):
            val = jnp.where(bucket == b, (rb_ref[b * DIFF_HEADS + h] - far) * LOG2_E, val)
        o_ref[0, which] = jnp.where(d >= 0, val, NEG_BIG)


def _t5_bias_tiles(rel_bias, tq):
    assert tq >= REL_MAX_DIST
    return pl.pallas_call(
        functools.partial(_t5_tiles_body, tq=tq),
        out_shape=(jax.ShapeDtypeStruct((DIFF_HEADS, 2, tq, tq), F32),
                   jax.ShapeDtypeStruct((DIFF_HEADS, 8, LANES), F32)),
        grid=(DIFF_HEADS,),
        in_specs=[pl.BlockSpec(memory_space=pltpu.SMEM)],
        out_specs=(pl.BlockSpec((1, 2, tq, tq), lambda h: (h, 0, 0, 0)),
                   pl.BlockSpec((1, 8, LANES), lambda h: (h, 0, 0))),
        compiler_params=_cparams(("parallel",)),
        name="t5_bias_tiles",
    )(rel_bias.reshape(-1))


def _diff_attn_body(lam_ref, qg_ref, kg_ref, sg_ref, bias_ref, bias_max_ref, q_ref, k_ref, v_ref, o_ref,
                    kn_s, m1, l1, a1, m2, l2, a2, *, tq, seq):
    qi = pl.program_id(2)
    lo = lax.broadcasted_iota(I32, (1, HEAD_W), 1) < DIFF_QK_DIM
    in_lo_r = lax.broadcasted_iota(I32, (HEAD_W, HEAD_W), 0) < DIFF_QK_DIM
    in_lo_c = lax.broadcasted_iota(I32, (HEAD_W, HEAD_W), 1) < DIFF_QK_DIM
    same_half = jnp.where(in_lo_r == in_lo_c, 1.0, 0.0).astype(BF16)
    same_half2 = jnp.concatenate([same_half, same_half], axis=0)

    def half_rms(x, g):
        x2 = x * x
        hi = x2.astype(BF16)
        lo2 = (x2 - hi.astype(F32)).astype(BF16)
        ss = jnp.dot(jnp.concatenate([hi, lo2], axis=1), same_half2, preferred_element_type=F32)
        return x * lax.rsqrt(ss * (1.0 / DIFF_QK_DIM) + EPS) * g

    @pl.when(qi == 0)
    def _():
        def norm_chunk(c, carry):
            rows = pl.ds(pl.multiple_of(c * tq, tq), tq)
            kn_s[rows, :] = half_rms(k_ref[rows, :].astype(F32), kg_ref[...]).astype(BF16)
            return carry
        lax.fori_loop(0, seq // tq, norm_chunk, 0, unroll=True)

    qn = half_rms(q_ref[...].astype(F32), qg_ref[...]) * (DIFF_QK_DIM ** -0.5 * LOG2_E)
    q1 = jnp.where(lo, qn, 0.0).astype(BF16)
    q2 = jnp.where(lo, 0.0, qn).astype(BF16)

    _softmax_init(m1, l1, a1)
    _softmax_init(m2, l2, a2)

    def block_rows(j):
        return pl.ds(pl.multiple_of(j * tq, tq), tq)

    def scores(j):
        kb = kn_s[block_rows(j), :]
        return (lax.dot_general(q1, kb, _NT, preferred_element_type=F32),
                lax.dot_general(q2, kb, _NT, preferred_element_type=F32))

    def sweep_all_blocks(softmax_step):
        def update(j, s, bias=None):
            s1, s2 = s if bias is None else (s[0] + bias, s[1] + bias)
            vb = v_ref[block_rows(j), :]
            softmax_step(s1, m1, l1, a1, vb)
            softmax_step(s2, m2, l2, a2, vb)

        _sweep_far_blocks(jnp.maximum(qi - 1, 0), scores, update)

        @pl.when(qi >= 1)
        def _():
            s_prev = scores(qi - 1)
            s_diag = scores(qi)
            update(qi - 1, s_prev, bias_ref[0, 1])
            update(qi, s_diag, bias_ref[0, 0])

        @pl.when(qi == 0)
        def _():
            update(0, scores(0), bias_ref[0, 0])

    bound = (_qk_norm_score_bound(DIFF_QK_DIM, DIFF_QK_DIM ** -0.5 * LOG2_E, qg_ref[...], kg_ref[...])
             + jnp.max(bias_max_ref[...]))
    bounded = 2.0 * bound <= BOUNDED_SOFTMAX_RANGE

    @pl.when(bounded)
    def _():
        sweep_all_blocks(lambda s, m, l, acc, v: _softmax_step_bounded(s, bound, l, acc, v))

    @pl.when(jnp.logical_not(bounded))
    def _():
        sweep_all_blocks(_softmax_step)

    lp = lam_ref[...]
    lam_init = lp[4:5, 0:1]
    lam = (jnp.exp(jnp.sum(lp[0:1] * lp[1:2], axis=-1, keepdims=True))
           - jnp.exp(jnp.sum(lp[2:3] * lp[3:4], axis=-1, keepdims=True)) + lam_init)
    o = _softmax_result(l1, a1) - lam * _softmax_result(l2, a2)
    ms = jnp.mean(o * o, axis=-1, keepdims=True)
    o = o * lax.rsqrt(ms + EPS) * sg_ref[...] * (1.0 - lam_init)
    o_ref[...] = o.astype(BF16)


def _diff_attention(proj, lam_tab, q_gain, k_gain, sub_gain, bias_tiles, bias_max, batch, seq, tq):
    T = proj.shape[0]
    nq = seq // tq
    H = DIFF_HEADS
    small = lambda b, h, i: (0, 0)
    return pl.pallas_call(
        functools.partial(_diff_attn_body, tq=tq, seq=seq),
        out_shape=jax.ShapeDtypeStruct((T, BRANCH_WIDTH), BF16),
        grid=(batch, H, nq),
        in_specs=[pl.BlockSpec((8, DIFF_QK_DIM), small),
                  pl.BlockSpec((1, HEAD_W), small),
                  pl.BlockSpec((1, HEAD_W), small),
                  pl.BlockSpec((1, HEAD_W), small),
                  pl.BlockSpec((1, 2, tq, tq), lambda b, h, i: (h, 0, 0, 0)),
                  pl.BlockSpec((1, 8, LANES), lambda b, h, i: (h, 0, 0)),
                  pl.BlockSpec((tq, HEAD_W), lambda b, h, i: (b * nq + i, h)),
                  pl.BlockSpec((seq, HEAD_W), lambda b, h, i: (b, H + h)),
                  pl.BlockSpec((seq, HEAD_W), lambda b, h, i: (b, 2 * H + h))],
        out_specs=pl.BlockSpec((tq, HEAD_W), lambda b, h, i: (b * nq + i, h)),
        scratch_shapes=[pltpu.VMEM((seq, HEAD_W), BF16),
                        pltpu.VMEM((tq, LANES), F32), pltpu.VMEM((tq, LANES), F32), pltpu.VMEM((tq, HEAD_W), F32),
                        pltpu.VMEM((tq, LANES), F32), pltpu.VMEM((tq, LANES), F32), pltpu.VMEM((tq, HEAD_W), F32)],
        compiler_params=_cparams(("parallel", "parallel", "arbitrary")),
        name="diff_attn",
    )(lam_tab, q_gain, k_gain, sub_gain, bias_tiles, bias_max, proj, proj, proj)


def _sb_attn_body(q_ref, k_ref, v_ref, o_ref, *scratch, sb, n_sub):
    qi = pl.program_id(2)
    row = lax.broadcasted_iota(I32, (sb, sb), 0)
    col = lax.broadcasted_iota(I32, (sb, sb), 1)
    tri = jnp.where(row >= col, 1.0, 0.0).astype(BF16)
    tri2 = jnp.concatenate([tri, tri], axis=0)
    strict = col < row
    n_rep = sb // LANES

    c_refs, acc_refs = scratch[:n_sub], scratch[n_sub:]
    for ref in scratch:
        ref[...] = jnp.zeros(ref.shape, F32)

    def sweep(it, masked):
        subs = range(n_sub)
        blks = [qi * n_sub + a - it for a in subs]
        rows = [pl.ds(pl.multiple_of(jnp.maximum(b, 0) * sb, sb), sb) for b in blks]
        zs = [lax.dot_general(q_ref[a * sb:(a + 1) * sb, :], k_ref[rows[a], :], _NT,
                              preferred_element_type=F32) * (SB_HEAD_DIM ** -0.5 * LOG2_E) for a in subs]
        log_fails, cums = [], []
        for a in subs:
            log_fail = -(jnp.maximum(zs[a], 0.0) + jnp.log2(1.0 + jnp.exp2(-jnp.abs(zs[a]))))
            if masked:
                log_fail = jnp.where(strict, log_fail, 0.0)
            hi = log_fail.astype(BF16)
            lo = (log_fail - hi.astype(F32)).astype(BF16)
            cums.append(jnp.dot(jnp.concatenate([hi, lo], axis=1), tri2, preferred_element_type=F32))
            log_fails.append(log_fail)
        c_max = jnp.float32(-jnp.inf)
        for a in subs:
            c = c_refs[a][...]
            dead = jnp.where(blks[a] >= 0, 0.0, NEG_BIG)
            w = jnp.exp2(zs[a] + cums[a] + _rep_lanes(c + dead, n_rep))
            if masked:
                w = jnp.where(strict, w, 0.0)
            acc_refs[a][...] += jnp.dot(w.astype(BF16), v_ref[rows[a], :], preferred_element_type=F32)
            c_new = c + jnp.sum(log_fails[a], axis=-1, keepdims=True)
            c_refs[a][...] = c_new
            c_max = jnp.maximum(c_max, jnp.where(blks[a] >= 1, jnp.max(c_new), -jnp.inf))
        return c_max

    c_max = sweep(0, True)

    def cond(carry):
        return carry[1] > SB_SKIP_LOG

    def body(carry):
        return carry[0] + 1, sweep(carry[0], False)

    lax.while_loop(cond, body, (jnp.int32(1), c_max))
    for a in range(n_sub):
        o_ref[a * sb:(a + 1) * sb, :] = acc_refs[a][...].astype(BF16)


def _sb_attention(proj, batch, seq, sb):
    T = proj.shape[0]
    n_sub = _tile(seq // sb, 4)
    tq = sb * n_sub
    nq = seq // tq
    H = SB_HEADS
    base = 3 * DIFF_HEADS
    return pl.pallas_call(
        functools.partial(_sb_attn_body, sb=sb, n_sub=n_sub),
        out_shape=jax.ShapeDtypeStruct((T, BRANCH_WIDTH), BF16),
        grid=(batch, H, nq),
        in_specs=[pl.BlockSpec((tq, HEAD_W), lambda b, h, i: (b * nq + i, base + h)),
                  pl.BlockSpec((seq, HEAD_W), lambda b, h, i: (b, base + H + h)),
                  pl.BlockSpec((seq, HEAD_W), lambda b, h, i: (b, base + 2 * H + h))],
        out_specs=pl.BlockSpec((tq, HEAD_W), lambda b, h, i: (b * nq + i, h)),
        scratch_shapes=([pltpu.VMEM((sb, LANES), F32)] * n_sub + [pltpu.VMEM((sb, HEAD_W), F32)] * n_sub),
        compiler_params=_cparams(("parallel", "parallel", "arbitrary")),
        name="sb_attn",
    )(proj, proj, proj)


def _mla_prep_body(pm_ref, qa_ref, kva_ref, wuq_ref, wukv_ref, qg_ref, kg_ref,
                   cos_ref, sina_ref, sinb_ref, q_ref, k_ref, v_ref):
    x = pm_ref[...].astype(F32)

    def rms(t, g):
        return t * lax.rsqrt(jnp.mean(t * t, axis=-1, keepdims=True) + EPS) * g

    cq = rms(x[:, :MLA_Q_RANK], qa_ref[...]).astype(BF16)
    ckv = rms(x[:, MLA_Q_RANK:MLA_Q_RANK + MLA_KV_RANK], kva_ref[...]).astype(BF16)
    kr = x[:, MLA_Q_RANK + MLA_KV_RANK:MLA_Q_RANK + MLA_KV_RANK + LANES]
    q = jnp.dot(cq, wuq_ref[...], preferred_element_type=F32)
    kv = jnp.dot(ckv, wukv_ref[...], preferred_element_type=F32)
    cos, sina, sinb = cos_ref[...], sina_ref[...], sinb_ref[...]
    half = MLA_ROPE_DIM // 2

    def rope(pe):
        return pe * cos + pltpu.roll(pe, LANES - half, 1) * sina + pltpu.roll(pe, half, 1) * sinb

    ones2 = jnp.ones((2 * LANES, LANES), BF16)

    def lane_sum(t):
        hi = t.astype(BF16)
        lo = (t - hi.astype(F32)).astype(BF16)
        return jnp.dot(jnp.concatenate([hi, lo], axis=1), ones2, preferred_element_type=F32)

    k_pe = rope(kr)
    k_pe_sq = k_pe * k_pe
    qk_dim = MLA_NOPE_DIM + MLA_ROPE_DIM
    qg, kg = qg_ref[...], kg_ref[...]
    for h in range(MLA_HEADS):
        o = h * MLA_QK_PAD
        q_n = q[:, o:o + MLA_NOPE_DIM]
        q_pe = rope(q[:, o + MLA_NOPE_DIM:o + MLA_QK_PAD])
        ss = lane_sum(q_n * q_n + q_pe * q_pe)
        r = lax.rsqrt(ss * (1.0 / qk_dim) + EPS) * (qk_dim ** -0.5 * LOG2_E)
        q_ref[:, o:o + MLA_NOPE_DIM] = (q_n * r * qg[:, :MLA_NOPE_DIM]).astype(BF16)
        q_ref[:, o + MLA_NOPE_DIM:o + MLA_QK_PAD] = (q_pe * r * qg[:, MLA_NOPE_DIM:]).astype(BF16)
        k_n = kv[:, o:o + MLA_NOPE_DIM]
        ssk = lane_sum(k_n * k_n + k_pe_sq)
        rk = lax.rsqrt(ssk * (1.0 / qk_dim) + EPS)
        k_ref[:, o:o + MLA_NOPE_DIM] = (k_n * rk * kg[:, :MLA_NOPE_DIM]).astype(BF16)
        k_ref[:, o + MLA_NOPE_DIM:o + MLA_QK_PAD] = (k_pe * rk * kg[:, MLA_NOPE_DIM:]).astype(BF16)
        v_ref[:, h * MLA_V_DIM:(h + 1) * MLA_V_DIM] = kv[:, o + MLA_NOPE_DIM:o + MLA_QK_PAD].astype(BF16)


def _mla_prep(proj, mla_blk, qa_g, kva_g, w_uq, w_ukv, q_g, k_g, cos_t, sina_t, sinb_t, seq):
    T = proj.shape[0]
    tm = _tile(seq, 512)
    ns = seq // tm
    W = MLA_HEADS * MLA_QK_PAD
    const = lambda i: (0, 0)
    pos = lambda i: (i % ns, 0)
    return pl.pallas_call(
        _mla_prep_body,
        out_shape=(jax.ShapeDtypeStruct((T, W), BF16), jax.ShapeDtypeStruct((T, W), BF16),
                   jax.ShapeDtypeStruct((T, MLA_HEADS * MLA_V_DIM), BF16)),
        grid=(T // tm,),
        in_specs=[pl.BlockSpec((tm, MLA_IN_W), lambda i: (i, mla_blk)),
                  pl.BlockSpec((1, MLA_Q_RANK), const),
                  pl.BlockSpec((1, MLA_KV_RANK), const),
                  pl.BlockSpec((MLA_Q_RANK, W), const),
                  pl.BlockSpec((MLA_KV_RANK, W), const),
                  pl.BlockSpec((1, MLA_QK_PAD), const),
                  pl.BlockSpec((1, MLA_QK_PAD), const),
                  pl.BlockSpec((tm, LANES), pos),
                  pl.BlockSpec((tm, LANES), pos),
                  pl.BlockSpec((tm, LANES), pos)],
        out_specs=(pl.BlockSpec((tm, W), lambda i: (i, 0)),
                   pl.BlockSpec((tm, W), lambda i: (i, 0)),
                   pl.BlockSpec((tm, MLA_HEADS * MLA_V_DIM), lambda i: (i, 0))),
        compiler_params=_cparams(("parallel",)),
        name="mla_prep",
    )(proj, qa_g, kva_g, w_uq, w_ukv, q_g, k_g, cos_t, sina_t, sinb_t)


def _mla_attn_body(qg_ref, kg_ref, q_ref, k_ref, v_ref, o_ref, *scratch, tq, n_heads):
    qi = pl.program_id(2)
    heads = range(n_heads)
    state = [scratch[3 * h:3 * h + 3] for h in heads]
    for m, l, acc in state:
        _softmax_init(m, l, acc)
    qs = [q_ref[:, h * MLA_QK_PAD:(h + 1) * MLA_QK_PAD] for h in heads]

    def block_rows(j):
        return pl.ds(pl.multiple_of(j * tq, tq), tq)

    def scores(j):
        return [lax.dot_general(qs[h], k_ref[block_rows(j), h * MLA_QK_PAD:(h + 1) * MLA_QK_PAD], _NT,
                                preferred_element_type=F32) for h in heads]

    causal = (lax.broadcasted_iota(I32, (tq, tq), 1) <= lax.broadcasted_iota(I32, (tq, tq), 0))

    def sweep_all_blocks(softmax_step):
        def update(j, s, masked=False):
            for h in heads:
                m, l, acc = state[h]
                s_h = jnp.where(causal, s[h], NEG_BIG) if masked else s[h]
                softmax_step(s_h, m, l, acc, v_ref[block_rows(j), h * MLA_V_DIM:(h + 1) * MLA_V_DIM])

        _sweep_far_blocks(jnp.maximum(qi - 1, 0), scores, update)

        @pl.when(qi >= 1)
        def _():
            s_prev = scores(qi - 1)
            s_diag = scores(qi)
            update(qi - 1, s_prev)
            update(qi, s_diag, masked=True)

        @pl.when(qi == 0)
        def _():
            update(0, scores(0), masked=True)

    qk_dim = MLA_NOPE_DIM + MLA_ROPE_DIM
    bound = _qk_norm_score_bound(qk_dim, qk_dim ** -0.5 * LOG2_E, qg_ref[...], kg_ref[...])
    bounded = 2.0 * bound <= BOUNDED_SOFTMAX_RANGE

    @pl.when(bounded)
    def _():
        sweep_all_blocks(lambda s, m_, l_, acc_, v: _softmax_step_bounded(s, bound, l_, acc_, v))

    @pl.when(jnp.logical_not(bounded))
    def _():
        sweep_all_blocks(_softmax_step)
    for h in heads:
        _, l, acc = state[h]
        o_ref[:, h * MLA_V_DIM:(h + 1) * MLA_V_DIM] = _softmax_result(l, acc).astype(BF16)


def _mla_attention(q, k, v, q_gain, k_gain, batch, seq, tq):
    T = q.shape[0]
    nq = seq // tq
    hp = 2
    gain_spec = pl.BlockSpec((1, MLA_QK_PAD), lambda b, h, i: (0, 0))
    return pl.pallas_call(
        functools.partial(_mla_attn_body, tq=tq, n_heads=hp),
        out_shape=jax.ShapeDtypeStruct((T, BRANCH_WIDTH), BF16),
        grid=(batch, MLA_HEADS // hp, nq),
        in_specs=[gain_spec, gain_spec,
                  pl.BlockSpec((tq, hp * MLA_QK_PAD), lambda b, h, i: (b * nq + i, h)),
                  pl.BlockSpec((seq, hp * MLA_QK_PAD), lambda b, h, i: (b, h)),
                  pl.BlockSpec((seq, hp * MLA_V_DIM), lambda b, h, i: (b, h))],
        out_specs=pl.BlockSpec((tq, hp * MLA_V_DIM), lambda b, h, i: (b * nq + i, h)),
        scratch_shapes=[pltpu.VMEM((tq, LANES), F32), pltpu.VMEM((tq, LANES), F32),
                        pltpu.VMEM((tq, MLA_V_DIM), F32)] * hp,
        compiler_params=_cparams(("parallel", "parallel", "arbitrary")),
        name="mla_attn",
    )(q_gain, k_gain, q, k, v)


def _merge_body(oa_ref, ob_ref, oc_ref, wb_ref, ga_ref, gb_ref, gc_ref, bg_ref, o_ref):
    acc = None
    for b, (o_r, g_r) in enumerate(((oa_ref, ga_ref), (ob_ref, gb_ref), (oc_ref, gc_ref))):
        gate = jax.nn.sigmoid(g_r[...].astype(F32) + bg_ref[b:b + 1, :])
        t = gate * jnp.dot(o_r[...], wb_ref[0, b], preferred_element_type=F32)
        acc = t if acc is None else acc + t
    o_ref[...] = acc.astype(BF16)


def _branch_merge(o_a, o_b, o_c, w_branch_all, layer, proj, b_gate, d_model):
    T = o_a.shape[0]
    tm, tn = _tile(T, 1024), _tile(d_model, 512)
    g0 = (QKV_W + MLA_IN_W) // tn
    nj = d_model // tn
    o_spec = pl.BlockSpec((tm, BRANCH_WIDTH), lambda i, j: (i, 0))
    gate_spec = lambda b: pl.BlockSpec((tm, tn), lambda i, j: (i, g0 + b * nj + j))
    return pl.pallas_call(
        _merge_body,
        out_shape=jax.ShapeDtypeStruct((T, d_model), BF16),
        grid=(T // tm, nj),
        in_specs=[o_spec, o_spec, o_spec,
                  pl.BlockSpec((1, N_BRANCH, BRANCH_WIDTH, tn), lambda i, j: (layer, 0, 0, j)),
                  gate_spec(0), gate_spec(1), gate_spec(2),
                  pl.BlockSpec((N_BRANCH, tn), lambda i, j: (0, j))],
        out_specs=pl.BlockSpec((tm, tn), lambda i, j: (i, j)),
        compiler_params=_cparams(("parallel", "arbitrary")),
        name="branch_merge",
    )(o_a, o_b, o_c, w_branch_all, proj, proj, proj, b_gate)


def _out_proj_body(a_ref, w_ref, r_ref, o_ref):
    o_ref[...] = r_ref[...] + jnp.dot(a_ref[...], w_ref[0], preferred_element_type=F32)


def _out_proj(merged, w_out_all, layer, x):
    T, K = merged.shape
    N = w_out_all.shape[2]
    tm, tn = _tile(T, 1024), _tile(N, 1024)
    return pl.pallas_call(
        _out_proj_body,
        out_shape=jax.ShapeDtypeStruct((T, N), F32),
        grid=(T // tm, N // tn),
        in_specs=[pl.BlockSpec((tm, K), lambda i, j: (i, 0)),
                  pl.BlockSpec((1, K, tn), lambda i, j: (layer, 0, j)),
                  pl.BlockSpec((tm, tn), lambda i, j: (i, j))],
        out_specs=pl.BlockSpec((tm, tn), lambda i, j: (i, j)),
        compiler_params=_cparams(("parallel", "arbitrary")),
        name="out_proj",
    )(merged, w_out_all, x)


def _route_body(x_ref, g_ref, wr_ref, br_ref, comb_ref, meta_ref, cnt_ref, carry_s, *, tm):
    i = pl.program_id(0)

    @pl.when(i == 0)
    def _():
        carry_s[...] = jnp.zeros(carry_s.shape, F32)

    x = x_ref[...]
    h = x * lax.rsqrt(jnp.mean(x * x, axis=-1, keepdims=True) + EPS) * g_ref[...]
    logits = jnp.dot(h, wr_ref[...], preferred_element_type=F32,
                     precision=lax.Precision.HIGHEST) + br_ref[...]
    lane = lax.broadcasted_iota(I32, (tm, LANES), 1)
    big = jnp.int32(LANES)

    gl = jnp.where(lane < N_GROUPS, logits, -jnp.inf)
    g_max = jnp.max(gl, axis=-1, keepdims=True)
    g_idx = jnp.min(jnp.where(gl == g_max, lane, big), axis=-1, keepdims=True)
    g_val = 1.0 / jnp.sum(jnp.exp(gl - g_max), axis=-1, keepdims=True)

    e_lo = N_GROUPS + g_idx * EXPERTS_PER_GROUP
    el = jnp.where(jnp.logical_and(lane >= e_lo, lane < e_lo + EXPERTS_PER_GROUP), logits, -jnp.inf)
    e1 = jnp.max(el, axis=-1, keepdims=True)
    i1 = jnp.min(jnp.where(el == e1, lane, big), axis=-1, keepdims=True)
    el2 = jnp.where(lane == i1, -jnp.inf, el)
    e2 = jnp.max(el2, axis=-1, keepdims=True)
    i2 = jnp.min(jnp.where(el2 == e2, lane, big), axis=-1, keepdims=True)
    t = jnp.exp(e2 - e1)
    w1 = g_val / (1.0 + t)
    w2 = g_val * t / (1.0 + t)
    comb_ref[...] = jnp.where(lane == i1, w1, 0.0) + jnp.where(lane == i2, w2, 0.0)

    onehot = jnp.where(lane == g_idx, 1.0, 0.0)
    r = lax.broadcasted_iota(I32, (tm, tm), 0)
    c = lax.broadcasted_iota(I32, (tm, tm), 1)
    before = jnp.where(c < r, 1.0, 0.0).astype(BF16)
    prefix = jnp.dot(before, onehot.astype(BF16), preferred_element_type=F32) + carry_s[...]
    rank = jnp.sum(jnp.where(lane == g_idx, prefix, 0.0), axis=-1, keepdims=True)
    carry_s[...] += jnp.sum(onehot, axis=0, keepdims=True)
    meta_ref[...] = jnp.where(lane == 0, g_idx, jnp.where(lane == 1, rank.astype(I32), 0))
    cnt_ref[...] = jnp.broadcast_to(carry_s[...], cnt_ref.shape).astype(I32)


def _moe_route(x, g, w_r, b_r):
    T, D = x.shape
    tm = _tile(T, 512)
    return pl.pallas_call(
        functools.partial(_route_body, tm=tm),
        out_shape=(jax.ShapeDtypeStruct((T, LANES), F32),
                   jax.ShapeDtypeStruct((T, LANES), I32), jax.ShapeDtypeStruct((8, LANES), I32)),
        grid=(T // tm,),
        in_specs=[pl.BlockSpec((tm, D), lambda i: (i, 0)),
                  pl.BlockSpec((1, D), lambda i: (0, 0)),
                  pl.BlockSpec((D, LANES), lambda i: (0, 0)),
                  pl.BlockSpec((1, LANES), lambda i: (0, 0))],
        out_specs=(pl.BlockSpec((tm, LANES), lambda i: (i, 0)),
                   pl.BlockSpec((tm, LANES), lambda i: (i, 0)),
                   pl.BlockSpec((8, LANES), lambda i: (0, 0))),
        scratch_shapes=[pltpu.VMEM((1, LANES), F32)],
        compiler_params=_cparams(("arbitrary",)),
        name="moe_route",
    )(x, g.reshape(1, D), w_r, b_r)


def _start_and_wait_rows(row_copy, n_rows):
    def issue(r, carry):
        row_copy(r).start()
        return carry
    lax.fori_loop(0, n_rows, issue, 0, unroll=ROW_DMA_UNROLL)

    def drain(r, carry):
        row_copy(r).wait()
        return carry
    lax.fori_loop(0, n_rows, drain, 0, unroll=ROW_DMA_UNROLL)


def _dispatch_body(dest_ref, x_ref, g_ref, comb_ref, xs_in, xs_ref, row_buf, sem, *, tm):
    del xs_in
    d_model = x_ref.shape[1]
    x = x_ref[...]
    row_buf[:, :d_model] = x * lax.rsqrt(jnp.mean(x * x, axis=-1, keepdims=True) + EPS) * g_ref[...]
    row_buf[:, d_model:] = comb_ref[...]
    base = pl.program_id(0) * tm

    def row_copy(r):
        return pltpu.make_async_copy(row_buf.at[pl.ds(r, 1)], xs_ref.at[pl.ds(dest_ref[base + r], 1)], sem.at[0])
    _start_and_wait_rows(row_copy, tm)


def _moe_dispatch(dest, x, g, comb, t_pad):
    T, D = x.shape
    W = D + LANES
    tm = _tile(T, ROW_DMA_TILE)
    return pl.pallas_call(
        functools.partial(_dispatch_body, tm=tm),
        out_shape=jax.ShapeDtypeStruct((t_pad, W), F32),
        grid_spec=pltpu.PrefetchScalarGridSpec(
            num_scalar_prefetch=1, grid=(T // tm,),
            in_specs=[pl.BlockSpec((tm, D), lambda i, d: (i, 0)),
                      pl.BlockSpec((1, D), lambda i, d: (0, 0)),
                      pl.BlockSpec((tm, LANES), lambda i, d: (i, 0)),
                      pl.BlockSpec(memory_space=pl.ANY)],
            out_specs=pl.BlockSpec(memory_space=pl.ANY),
            scratch_shapes=[pltpu.VMEM((tm, W), F32), pltpu.SemaphoreType.DMA((1,))]),
        input_output_aliases={4: 0},
        compiler_params=_cparams(("arbitrary",)),
        name="moe_dispatch",
    )(dest, x, g.reshape(1, D), comb, jnp.zeros((t_pad, W), F32))


def _gather_rows_into(idx_ref, table_ref, dst_ref, sem, n_rows):
    base = pl.program_id(0) * n_rows

    def row_copy(r):
        return pltpu.make_async_copy(table_ref.at[pl.ds(idx_ref[base + r], 1)], dst_ref.at[pl.ds(r, 1)], sem.at[0])
    _start_and_wait_rows(row_copy, n_rows)


def _experts_body(tg_ref, nt_ref, xs_ref, wgu_ref, wd_ref, y_ref, *, d_expert):
    i, e = pl.program_id(0), pl.program_id(1)
    valid = i < nt_ref[0]
    d_model = y_ref.shape[1]

    @pl.when(valid)
    def _():
        x = xs_ref[:, :d_model].astype(BF16)
        gu = jnp.dot(x, wgu_ref[0], preferred_element_type=F32)
        comb = xs_ref[:, d_model:]
        lane = lax.broadcasted_iota(I32, comb.shape, 1)
        sel = N_GROUPS + tg_ref[i] * EXPERTS_PER_GROUP + e
        c = jnp.sum(jnp.where(lane == sel, comb, 0.0), axis=-1, keepdims=True)
        a = (jax.nn.silu(gu[:, :d_expert]) * gu[:, d_expert:] * c).astype(BF16)
        y = jnp.dot(a, wd_ref[0], preferred_element_type=F32)

        @pl.when(e == 0)
        def _():
            y_ref[...] = y

        @pl.when(e > 0)
        def _():
            y_ref[...] += y

    @pl.when(jnp.logical_and(jnp.logical_not(valid), e == 0))
    def _():
        y_ref[...] = jnp.zeros(y_ref.shape, F32)


def _moe_experts(tile_group, n_tiles, xs, w_gu, w_down, layer, tg):
    t_pad = xs.shape[0]
    d_expert, D = w_down.shape[1:]
    expert = lambda i, e, tgr, ntr: (layer * N_EXPERTS + tgr[i] * EXPERTS_PER_GROUP + e, 0, 0)
    rows = lambda i, e, tgr, ntr: (i, 0)
    return pl.pallas_call(
        functools.partial(_experts_body, d_expert=d_expert),
        out_shape=jax.ShapeDtypeStruct((t_pad, D), F32),
        grid_spec=pltpu.PrefetchScalarGridSpec(
            num_scalar_prefetch=2, grid=(t_pad // tg, EXPERTS_PER_GROUP),
            in_specs=[pl.BlockSpec((tg, D + LANES), rows),
                      pl.BlockSpec((1, D, 2 * d_expert), expert),
                      pl.BlockSpec((1, d_expert, D), expert)],
            out_specs=pl.BlockSpec((tg, D), rows)),
        compiler_params=_cparams(("parallel", "arbitrary")),
        name="moe_experts",
    )(tile_group, n_tiles, xs, w_gu, w_down)


def _combine_body(dest_ref, x_ref, ys_ref, o_ref, buf, sem, *, tm):
    _gather_rows_into(dest_ref, ys_ref, buf, sem, tm)
    o_ref[...] = x_ref[...] + buf[...]


def _moe_combine(dest, x, ys):
    T, D = x.shape
    tm = _tile(T, ROW_DMA_TILE)
    return pl.pallas_call(
        functools.partial(_combine_body, tm=tm),
        out_shape=jax.ShapeDtypeStruct((T, D), F32),
        grid_spec=pltpu.PrefetchScalarGridSpec(
            num_scalar_prefetch=1, grid=(T // tm,),
            in_specs=[pl.BlockSpec((tm, D), lambda i, d: (i, 0)),
                      pl.BlockSpec(memory_space=pl.ANY)],
            out_specs=pl.BlockSpec((tm, D), lambda i, d: (i, 0)),
            scratch_shapes=[pltpu.VMEM((tm, D), F32), pltpu.SemaphoreType.DMA((1,))]),
        compiler_params=_cparams(("arbitrary",)),
        name="moe_combine",
    )(dest, x, ys)


def _hier_moe(x, ln_g, w_r, b_r, w_gu, w_down, layer):
    T, D = x.shape
    tg = _tile(T, 512)
    comb, meta, counts = _moe_route(x, ln_g, w_r, b_r)
    cnt = counts[0, :N_GROUPS]
    seg_tiles = (cnt + tg - 1) // tg
    tile_end = jnp.cumsum(seg_tiles)
    seg_off = (tile_end - seg_tiles) * tg
    dest = seg_off[meta[:, 0]] + meta[:, 1]
    n_tiles_max = T // tg + N_GROUPS
    n_tiles = tile_end[-1]
    tile_ids = jnp.arange(n_tiles_max, dtype=I32)
    tile_group = jnp.searchsorted(tile_end, jnp.minimum(tile_ids, n_tiles - 1), side="right").astype(I32)
    dest = dest.astype(I32)
    xs = _moe_dispatch(dest, x, ln_g, comb, n_tiles_max * tg)
    ys = _moe_experts(tile_group, n_tiles.reshape(1).astype(I32), xs, w_gu, w_down, layer, tg)
    return _moe_combine(dest, x, ys)


def _layout_w_in(w_in):
    mla_w = MLA_Q_RANK + MLA_KV_RANK + MLA_ROPE_DIM
    w_a = jnp.pad(w_in[:, :, :QKV_W + mla_w].astype(BF16), ((0, 0), (0, 0), (0, MLA_IN_W - mla_w)))
    w_b = w_in[:, :, QKV_W + mla_w:].astype(BF16)
    return w_a, w_b


def _layout_w_uq(w_uq):
    w = w_uq.reshape(MLA_Q_RANK, MLA_HEADS, MLA_NOPE_DIM + MLA_ROPE_DIM)
    w = jnp.pad(w, ((0, 0), (0, 0), (0, MLA_QK_PAD - MLA_NOPE_DIM - MLA_ROPE_DIM)))
    return w.reshape(MLA_Q_RANK, MLA_HEADS * MLA_QK_PAD).astype(BF16)


def _pad_gain(g):
    return jnp.pad(g, (0, MLA_QK_PAD - g.shape[0])).reshape(1, MLA_QK_PAD)


def _rope_tables(seq):
    pos = jnp.arange(seq, dtype=F32)
    inv_freq = 1.0 / (ROPE_THETA ** (jnp.arange(0, MLA_ROPE_DIM, 2, dtype=F32) / MLA_ROPE_DIM))
    ang = pos[:, None] * inv_freq[None, :]
    cos, sin = jnp.cos(ang), jnp.sin(ang)
    z = jnp.zeros_like(cos)
    return (jnp.concatenate([cos, cos, z, z], axis=1),
            jnp.concatenate([-sin, z, z, z], axis=1),
            jnp.concatenate([z, sin, z, z], axis=1))


def kernel(x, rel_bias, ln1_g, w_in, diff_q_norm, diff_k_norm, diff_lambda, diff_subln_g, mla_q_a_norm, mla_w_uq, mla_kv_a_norm, mla_w_ukv, mla_q_norm, mla_k_norm, w_branch, b_gate, w_out, ln2_g, router_group_w, router_group_b, router_expert_w, router_expert_b, expert_w_gu, expert_w_down):
    B, S, D = x.shape
    depth = w_in.shape[0]
    T = B * S
    tq = _tile(S, 512)
    tq_sb = _tile(S, 256)
    cos_t, sina_t, sinb_t = _rope_tables(S)
    bias_tiles, bias_max = _t5_bias_tiles(rel_bias, tq)
    w_in_a, w_in_b = _layout_w_in(w_in)
    mla_blk = QKV_W // MLA_IN_W
    w_branch_all = w_branch.astype(BF16)
    w_out_all = w_out.astype(BF16)
    w_gu_all = expert_w_gu.astype(BF16).reshape((depth * N_EXPERTS,) + expert_w_gu.shape[2:])
    w_down_all = expert_w_down.astype(BF16).reshape((depth * N_EXPERTS,) + expert_w_down.shape[2:])
    xt = x.reshape(T, D)
    for l in range(depth):
        proj = _rms_matmul(xt, ln1_g[l], w_in_a, w_in_b, l, BF16, "rms_in_proj")

        lam_init = 0.8 - 0.6 * math.exp(-0.3 * l)
        lam_tab = jnp.concatenate([diff_lambda[l], jnp.full((4, DIFF_QK_DIM), lam_init, F32)], axis=0)
        o_a = _diff_attention(proj, lam_tab,
                              jnp.tile(diff_q_norm[l], 2).reshape(1, HEAD_W),
                              jnp.tile(diff_k_norm[l], 2).reshape(1, HEAD_W),
                              diff_subln_g[l].reshape(1, HEAD_W), bias_tiles, bias_max, B, S, tq)
        o_b = _sb_attention(proj, B, S, tq_sb)
        mla_qg, mla_kg = _pad_gain(mla_q_norm[l]), _pad_gain(mla_k_norm[l])
        q_c, k_c, v_c = _mla_prep(proj, mla_blk, mla_q_a_norm[l].reshape(1, -1), mla_kv_a_norm[l].reshape(1, -1),
                                  _layout_w_uq(mla_w_uq[l]), mla_w_ukv[l].astype(BF16),
                                  mla_qg, mla_kg, cos_t, sina_t, sinb_t, S)
        o_c = _mla_attention(q_c, k_c, v_c, mla_qg, mla_kg, B, S, tq)

        merged = _branch_merge(o_a, o_b, o_c, w_branch_all, l, proj, b_gate[l].reshape(N_BRANCH, D), D)
        xt = _out_proj(merged, w_out_all, l, xt)

        w_r = jnp.concatenate([router_group_w[l], router_expert_w[l],
                               jnp.zeros((D, LANES - N_GROUPS - N_EXPERTS), F32)], axis=1)
        b_r = jnp.concatenate([router_group_b[l], router_expert_b[l],
                               jnp.zeros((LANES - N_GROUPS - N_EXPERTS,), F32)]).reshape(1, LANES)
        xt = _hier_moe(xt, ln2_g[l], w_r, b_r, w_gu_all, w_down_all, l)
    return xt.reshape(B, S, D)
```

```python
import functools
import math

import jax
import jax.numpy as jnp
from jax import lax
from jax.experimental import pallas as pl
from jax.experimental.pallas import tpu as pltpu

F32 = jnp.float32
BF16 = jnp.bfloat16
I32 = jnp.int32

EPS = 1e-6
LANES = 128
HEAD_W = 128
DIFF_HEADS = 8
DIFF_QK_DIM = 64
SB_HEADS = 8
SB_HEAD_DIM = 128
MLA_HEADS = 8
MLA_Q_RANK = 512
MLA_KV_RANK = 256
MLA_NOPE_DIM = 128
MLA_ROPE_DIM = 64
MLA_V_DIM = 128
MLA_QK_PAD = 256
ROPE_THETA = 10000.0
REL_BUCKETS = 32
REL_MAX_DIST = 128
N_GROUPS = 8
EXPERTS_PER_GROUP = 4
N_EXPERTS = N_GROUPS * EXPERTS_PER_GROUP
N_BRANCH = 3
BRANCH_WIDTH = 1024
QKV_W = 6 * BRANCH_WIDTH
MLA_IN_W = 1024
NEG_BIG = -1e30
LOG2_E = math.log2(math.e)
SB_SKIP_LOG = -100.0 * LOG2_E
VMEM_LIMIT = 56 * 1024 * 1024
BOUND_MARGIN = 1.02
BOUNDED_SOFTMAX_RANGE = 96.0
ROW_DMA_TILE = 512
ROW_DMA_UNROLL = 8

_NT = (((1,), (1,)), ((), ()))


def _cparams(sem, vmem=VMEM_LIMIT):
    return pltpu.CompilerParams(dimension_semantics=sem, vmem_limit_bytes=vmem)


def _tile(n, target):
    t = min(n, target)
    while n % t:
        t -= 1
    return t


def _round_up(n, m):
    return -(-n // m) * m


def _rms_mm_body(x_ref, g_ref, wa_ref, wb_ref, o_ref, h_ref, *, n_a):
    j = pl.program_id(1)

    @pl.when(j == 0)
    def _():
        x = x_ref[...]
        ms = jnp.mean(x * x, axis=-1, keepdims=True)
        h_ref[...] = (x * lax.rsqrt(ms + EPS) * g_ref[...]).astype(BF16)

    @pl.when(j < n_a)
    def _():
        o_ref[...] = jnp.dot(h_ref[...], wa_ref[0], preferred_element_type=F32).astype(o_ref.dtype)

    @pl.when(j >= n_a)
    def _():
        o_ref[...] = jnp.dot(h_ref[...], wb_ref[0], preferred_element_type=F32).astype(o_ref.dtype)


def _rms_matmul(x, g, wa_all, wb_all, layer, out_dtype, name):
    T, K = x.shape
    na_cols, nb_cols = wa_all.shape[2], wb_all.shape[2]
    tm, tn = _tile(T, 1024), _tile(math.gcd(na_cols, nb_cols), 1024)
    n_a, n_b = na_cols // tn, nb_cols // tn
    return pl.pallas_call(
        functools.partial(_rms_mm_body, n_a=n_a),
        out_shape=jax.ShapeDtypeStruct((T, na_cols + nb_cols), out_dtype),
        grid=(T // tm, n_a + n_b),
        in_specs=[pl.BlockSpec((tm, K), lambda i, j: (i, 0)),
                  pl.BlockSpec((1, K), lambda i, j: (0, 0)),
                  pl.BlockSpec((1, K, tn), lambda i, j: (layer, 0, jnp.minimum(j, n_a - 1))),
                  pl.BlockSpec((1, K, tn), lambda i, j: (layer, 0, jnp.where(j < n_a, n_b - 1, j - n_a)))],
        out_specs=pl.BlockSpec((tm, tn), lambda i, j: (i, j)),
        scratch_shapes=[pltpu.VMEM((tm, K), BF16)],
        compiler_params=_cparams(("parallel", "arbitrary")),
        name=name,
    )(x, g.reshape(1, K), wa_all, wb_all)


def _rep_lanes(x, n):
    return jnp.concatenate([x] * n, axis=1)


def _softmax_step(s, m_ref, l_ref, acc_ref, v):
    n_rep = s.shape[1] // LANES
    m_prev = m_ref[...]
    m_new = jnp.maximum(m_prev, jnp.max(s, axis=-1, keepdims=True))
    alpha = jnp.exp2(m_prev - m_new)
    p = jnp.exp2(s - _rep_lanes(m_new, n_rep))
    p_sum = p[:, :LANES]
    for c in range(1, n_rep):
        p_sum = p_sum + p[:, c * LANES:(c + 1) * LANES]
    l_ref[...] = alpha * l_ref[...] + p_sum
    acc_ref[...] = alpha * acc_ref[...] + jnp.dot(p.astype(BF16), v, preferred_element_type=F32)
    m_ref[...] = m_new


def _softmax_step_bounded(s, shift, l_ref, acc_ref, v):
    n_rep = s.shape[1] // LANES
    p = jnp.exp2(s - shift)
    p_sum = p[:, :LANES]
    for c in range(1, n_rep):
        p_sum = p_sum + p[:, c * LANES:(c + 1) * LANES]
    l_ref[...] += p_sum
    acc_ref[...] += jnp.dot(p.astype(BF16), v, preferred_element_type=F32)


def _qk_norm_score_bound(dim, scale, q_gain, k_gain):
    return dim * scale * jnp.max(jnp.abs(q_gain)) * jnp.max(jnp.abs(k_gain)) * BOUND_MARGIN


def _softmax_result(l_ref, acc_ref):
    return acc_ref[...] / jnp.sum(l_ref[...], axis=-1, keepdims=True)


def _sweep_far_blocks(n_far, scores, update):
    def pair(p, carry):
        s_a = scores(2 * p)
        s_b = scores(2 * p + 1)
        update(2 * p, s_a)
        update(2 * p + 1, s_b)
        return carry
    lax.fori_loop(0, n_far // 2, pair, 0)

    @pl.when(n_far % 2 == 1)
    def _():
        update(n_far - 1, scores(n_far - 1))


def _softmax_init(m_ref, l_ref, acc_ref):
    m_ref[...] = jnp.full(m_ref.shape, NEG_BIG, F32)
    l_ref[...] = jnp.zeros(l_ref.shape, F32)
    acc_ref[...] = jnp.zeros(acc_ref.shape, F32)


def _t5_tiles_body(rb_ref, o_ref, max_ref, *, tq):
    h = pl.program_id(0)
    row = lax.broadcasted_iota(I32, (tq, tq), 0)
    col = lax.broadcasted_iota(I32, (tq, tq), 1)
    max_exact = REL_BUCKETS // 2
    far = rb_ref[(REL_BUCKETS - 1) * DIFF_HEADS + h]
    bias_max = jnp.float32(0.0)
    for b in range(REL_BUCKETS - 1):
        bias_max = jnp.maximum(bias_max, (rb_ref[b * DIFF_HEADS + h] - far) * LOG2_E)
    max_ref[...] = jnp.full(max_ref.shape, bias_max, F32)
    for which in range(2):
        d = row - col + which * tq
        n = jnp.maximum(d, 0)
        nf = jnp.maximum(n, 1).astype(F32)
        large = max_exact + (jnp.log(nf / max_exact) / math.log(REL_MAX_DIST / max_exact)
                             * (REL_BUCKETS - max_exact)).astype(I32)
        large = jnp.minimum(large, REL_BUCKETS - 1)
        bucket = jnp.where(n < max_exact, n, large)
        val = jnp.zeros((tq, tq), F32)
        for b in range(REL_BUCKETS - 1):
            val = jnp.where(bucket == b, (rb_ref[b * DIFF_HEADS + h] - far) * LOG2_E, val)
        o_ref[0, which] = jnp.where(d >= 0, val, NEG_BIG)


def _t5_bias_tiles(rel_bias, tq):
    assert tq >= REL_MAX_DIST
    return pl.pallas_call(
        functools.partial(_t5_tiles_body, tq=tq),
        out_shape=(jax.ShapeDtypeStruct((DIFF_HEADS, 2, tq, tq), F32),
                   jax.ShapeDtypeStruct((DIFF_HEADS, 8, LANES), F32)),
        grid=(DIFF_HEADS,),
        in_specs=[pl.BlockSpec(memory_space=pltpu.SMEM)],
        out_specs=(pl.BlockSpec((1, 2, tq, tq), lambda h: (h, 0, 0, 0)),
                   pl.BlockSpec((1, 8, LANES), lambda h: (h, 0, 0))),
        compiler_params=_cparams(("parallel",)),
        name="t5_bias_tiles",
    )(rel_bias.reshape(-1))


def _diff_attn_body(lam_ref, qg_ref, kg_ref, sg_ref, bias_ref, bias_max_ref, q_ref, k_ref, v_ref, o_ref,
                    *scratch, tq, seq, n_heads):
    qi = pl.program_id(2)
    heads = range(n_heads)
    kn = [scratch[7 * h] for h in heads]
    state = [scratch[7 * h + 1:7 * h + 7] for h in heads]

    def cols(h):
        return slice(h * HEAD_W, (h + 1) * HEAD_W)

    lo = lax.broadcasted_iota(I32, (1, HEAD_W), 1) < DIFF_QK_DIM
    in_lo_r = lax.broadcasted_iota(I32, (HEAD_W, HEAD_W), 0) < DIFF_QK_DIM
    in_lo_c = lax.broadcasted_iota(I32, (HEAD_W, HEAD_W), 1) < DIFF_QK_DIM
    same_half = jnp.where(in_lo_r == in_lo_c, 1.0, 0.0).astype(BF16)
    same_half2 = jnp.concatenate([same_half, same_half], axis=0)

    def half_rms(x, g):
        x2 = x * x
        hi = x2.astype(BF16)
        lo2 = (x2 - hi.astype(F32)).astype(BF16)
        ss = jnp.dot(jnp.concatenate([hi, lo2], axis=1), same_half2, preferred_element_type=F32)
        return x * lax.rsqrt(ss * (1.0 / DIFF_QK_DIM) + EPS) * g

    @pl.when(qi == 0)
    def _():
        def norm_chunk(c, carry):
            rows = pl.ds(pl.multiple_of(c * tq, tq), tq)
            for h in heads:
                kn[h][rows, :] = half_rms(k_ref[rows, cols(h)].astype(F32), kg_ref[...]).astype(BF16)
            return carry
        lax.fori_loop(0, seq // tq, norm_chunk, 0, unroll=True)

    q1, q2 = [], []
    for h in heads:
        qn = half_rms(q_ref[:, cols(h)].astype(F32), qg_ref[...]) * (DIFF_QK_DIM ** -0.5 * LOG2_E)
        q1.append(jnp.where(lo, qn, 0.0).astype(BF16))
        q2.append(jnp.where(lo, 0.0, qn).astype(BF16))
        _softmax_init(*state[h][:3])
        _softmax_init(*state[h][3:])

    def block_rows(j):
        return pl.ds(pl.multiple_of(j * tq, tq), tq)

    def scores(j):
        out = []
        for h in heads:
            kb = kn[h][block_rows(j), :]
            out.append((lax.dot_general(q1[h], kb, _NT, preferred_element_type=F32),
                        lax.dot_general(q2[h], kb, _NT, preferred_element_type=F32)))
        return out

    def sweep_all_blocks(softmax_step):
        def update(j, s, bias_tile=None):
            for h in heads:
                m1, l1, a1, m2, l2, a2 = state[h]
                s1, s2 = s[h]
                if bias_tile is not None:
                    bias = bias_ref[h, bias_tile]
                    s1, s2 = s1 + bias, s2 + bias
                vb = v_ref[block_rows(j), cols(h)]
                softmax_step(s1, m1, l1, a1, vb)
                softmax_step(s2, m2, l2, a2, vb)

        _sweep_far_blocks(jnp.maximum(qi - 1, 0), scores, update)

        @pl.when(qi >= 1)
        def _():
            s_prev = scores(qi - 1)
            s_diag = scores(qi)
            update(qi - 1, s_prev, 1)
            update(qi, s_diag, 0)

        @pl.when(qi == 0)
        def _():
            update(0, scores(0), 0)

    bound = (_qk_norm_score_bound(DIFF_QK_DIM, DIFF_QK_DIM ** -0.5 * LOG2_E, qg_ref[...], kg_ref[...])
             + jnp.max(bias_max_ref[...]))
    bounded = 2.0 * bound <= BOUNDED_SOFTMAX_RANGE

    @pl.when(bounded)
    def _():
        sweep_all_blocks(lambda s, m, l, acc, v: _softmax_step_bounded(s, bound, l, acc, v))

    @pl.when(jnp.logical_not(bounded))
    def _():
        sweep_all_blocks(_softmax_step)

    lp = lam_ref[...]
    lam_init = lp[4:5, 0:1]
    lam = (jnp.exp(jnp.sum(lp[0:1] * lp[1:2], axis=-1, keepdims=True))
           - jnp.exp(jnp.sum(lp[2:3] * lp[3:4], axis=-1, keepdims=True)) + lam_init)
    for h in heads:
        _, l1, a1, _, l2, a2 = state[h]
        o = _softmax_result(l1, a1) - lam * _softmax_result(l2, a2)
        ms = jnp.mean(o * o, axis=-1, keepdims=True)
        o = o * lax.rsqrt(ms + EPS) * sg_ref[...] * (1.0 - lam_init)
        o_ref[:, cols(h)] = o.astype(BF16)


def _diff_attention(proj, lam_tab, q_gain, k_gain, sub_gain, bias_tiles, bias_max, batch, seq, tq):
    T = proj.shape[0]
    nq = seq // tq
    hp = 2
    hb = DIFF_HEADS // hp
    small = lambda b, h, i: (0, 0)
    head_scratch = [pltpu.VMEM((seq, HEAD_W), BF16),
                    pltpu.VMEM((tq, LANES), F32), pltpu.VMEM((tq, LANES), F32), pltpu.VMEM((tq, HEAD_W), F32),
                    pltpu.VMEM((tq, LANES), F32), pltpu.VMEM((tq, LANES), F32), pltpu.VMEM((tq, HEAD_W), F32)]
    return pl.pallas_call(
        functools.partial(_diff_attn_body, tq=tq, seq=seq, n_heads=hp),
        out_shape=jax.ShapeDtypeStruct((T, BRANCH_WIDTH), BF16),
        grid=(batch, hb, nq),
        in_specs=[pl.BlockSpec((8, DIFF_QK_DIM), small),
                  pl.BlockSpec((1, HEAD_W), small),
                  pl.BlockSpec((1, HEAD_W), small),
                  pl.BlockSpec((1, HEAD_W), small),
                  pl.BlockSpec((hp, 2, tq, tq), lambda b, h, i: (h, 0, 0, 0)),
                  pl.BlockSpec((hp, 8, LANES), lambda b, h, i: (h, 0, 0)),
                  pl.BlockSpec((tq, hp * HEAD_W), lambda b, h, i: (b * nq + i, h)),
                  pl.BlockSpec((seq, hp * HEAD_W), lambda b, h, i: (b, hb + h)),
                  pl.BlockSpec((seq, hp * HEAD_W), lambda b, h, i: (b, 2 * hb + h))],
        out_specs=pl.BlockSpec((tq, hp * HEAD_W), lambda b, h, i: (b * nq + i, h)),
        scratch_shapes=head_scratch * hp,
        compiler_params=_cparams(("parallel", "parallel", "arbitrary")),
        name="diff_attn",
    )(lam_tab, q_gain, k_gain, sub_gain, bias_tiles, bias_max, proj, proj, proj)


def _sb_attn_body(q_ref, k_ref, v_ref, o_ref, *scratch, sb, n_sub):
    qi = pl.program_id(2)
    row = lax.broadcasted_iota(I32, (sb, sb), 0)
    col = lax.broadcasted_iota(I32, (sb, sb), 1)
    tri = jnp.where(row >= col, 1.0, 0.0).astype(BF16)
    tri2 = jnp.concatenate([tri, tri], axis=0)
    strict = col < row
    n_rep = sb // LANES

    c_refs, acc_refs = scratch[:n_sub], scratch[n_sub:]
    for ref in scratch:
        ref[...] = jnp.zeros(ref.shape, F32)

    def sweep(it, masked):
        subs = range(n_sub)
        blks = [qi * n_sub + a - it for a in subs]
        rows = [pl.ds(pl.multiple_of(jnp.maximum(b, 0) * sb, sb), sb) for b in blks]
        zs = [lax.dot_general(q_ref[a * sb:(a + 1) * sb, :], k_ref[rows[a], :], _NT,
                              preferred_element_type=F32) * (SB_HEAD_DIM ** -0.5 * LOG2_E) for a in subs]
        log_fails, cums = [], []
        for a in subs:
            log_fail = -(jnp.maximum(zs[a], 0.0) + jnp.log2(1.0 + jnp.exp2(-jnp.abs(zs[a]))))
            if masked:
                log_fail = jnp.where(strict, log_fail, 0.0)
            hi = log_fail.astype(BF16)
            lo = (log_fail - hi.astype(F32)).astype(BF16)
            cums.append(jnp.dot(jnp.concatenate([hi, lo], axis=1), tri2, preferred_element_type=F32))
            log_fails.append(log_fail)
        c_max = jnp.float32(-jnp.inf)
        for a in subs:
            c = c_refs[a][...]
            dead = jnp.where(blks[a] >= 0, 0.0, NEG_BIG)
            w = jnp.exp2(zs[a] + cums[a] + _rep_lanes(c + dead, n_rep))
            if masked:
                w = jnp.where(strict, w, 0.0)
            acc_refs[a][...] += jnp.dot(w.astype(BF16), v_ref[rows[a], :], preferred_element_type=F32)
            c_new = c + jnp.sum(log_fails[a], axis=-1, keepdims=True)
            c_refs[a][...] = c_new
            c_max = jnp.maximum(c_max, jnp.where(blks[a] >= 1, jnp.max(c_new), -jnp.inf))
        return c_max

    c_max = sweep(0, True)

    def cond(carry):
        return carry[1] > SB_SKIP_LOG

    def body(carry):
        return carry[0] + 1, sweep(carry[0], False)

    lax.while_loop(cond, body, (jnp.int32(1), c_max))
    for a in range(n_sub):
        o_ref[a * sb:(a + 1) * sb, :] = acc_refs[a][...].astype(BF16)


def _sb_attention(proj, batch, seq, sb):
    T = proj.shape[0]
    n_sub = _tile(seq // sb, 4)
    tq = sb * n_sub
    nq = seq // tq
    H = SB_HEADS
    base = 3 * DIFF_HEADS
    return pl.pallas_call(
        functools.partial(_sb_attn_body, sb=sb, n_sub=n_sub),
        out_shape=jax.ShapeDtypeStruct((T, BRANCH_WIDTH), BF16),
        grid=(batch, H, nq),
        in_specs=[pl.BlockSpec((tq, HEAD_W), lambda b, h, i: (b * nq + i, base + h)),
                  pl.BlockSpec((seq, HEAD_W), lambda b, h, i: (b, base + H + h)),
                  pl.BlockSpec((seq, HEAD_W), lambda b, h, i: (b, base + 2 * H + h))],
        out_specs=pl.BlockSpec((tq, HEAD_W), lambda b, h, i: (b * nq + i, h)),
        scratch_shapes=([pltpu.VMEM((sb, LANES), F32)] * n_sub + [pltpu.VMEM((sb, HEAD_W), F32)] * n_sub),
        compiler_params=_cparams(("parallel", "parallel", "arbitrary")),
        name="sb_attn",
    )(proj, proj, proj)


def _mla_prep_body(pm_ref, qa_ref, kva_ref, wuq_ref, wukv_ref, qg_ref, kg_ref,
                   cos_ref, sina_ref, sinb_ref, q_ref, k_ref, v_ref):
    x = pm_ref[...].astype(F32)

    def rms(t, g):
        return t * lax.rsqrt(jnp.mean(t * t, axis=-1, keepdims=True) + EPS) * g

    cq = rms(x[:, :MLA_Q_RANK], qa_ref[...]).astype(BF16)
    ckv = rms(x[:, MLA_Q_RANK:MLA_Q_RANK + MLA_KV_RANK], kva_ref[...]).astype(BF16)
    kr = x[:, MLA_Q_RANK + MLA_KV_RANK:MLA_Q_RANK + MLA_KV_RANK + LANES]
    q = jnp.dot(cq, wuq_ref[...], preferred_element_type=F32)
    kv = jnp.dot(ckv, wukv_ref[...], preferred_element_type=F32)
    cos, sina, sinb = cos_ref[...], sina_ref[...], sinb_ref[...]
    half = MLA_ROPE_DIM // 2

    def rope(pe):
        return pe * cos + pltpu.roll(pe, LANES - half, 1) * sina + pltpu.roll(pe, half, 1) * sinb

    ones2 = jnp.ones((2 * LANES, LANES), BF16)

    def lane_sum(t):
        hi = t.astype(BF16)
        lo = (t - hi.astype(F32)).astype(BF16)
        return jnp.dot(jnp.concatenate([hi, lo], axis=1), ones2, preferred_element_type=F32)

    k_pe = rope(kr)
    k_pe_sq = k_pe * k_pe
    qk_dim = MLA_NOPE_DIM + MLA_ROPE_DIM
    qg, kg = qg_ref[...], kg_ref[...]
    for h in range(MLA_HEADS):
        o = h * MLA_QK_PAD
        q_n = q[:, o:o + MLA_NOPE_DIM]
        q_pe = rope(q[:, o + MLA_NOPE_DIM:o + MLA_QK_PAD])
        ss = lane_sum(q_n * q_n + q_pe * q_pe)
        r = lax.rsqrt(ss * (1.0 / qk_dim) + EPS) * (qk_dim ** -0.5 * LOG2_E)
        q_ref[:, o:o + MLA_NOPE_DIM] = (q_n * r * qg[:, :MLA_NOPE_DIM]).astype(BF16)
        q_ref[:, o + MLA_NOPE_DIM:o + MLA_QK_PAD] = (q_pe * r * qg[:, MLA_NOPE_DIM:]).astype(BF16)
        k_n = kv[:, o:o + MLA_NOPE_DIM]
        ssk = lane_sum(k_n * k_n + k_pe_sq)
        rk = lax.rsqrt(ssk * (1.0 / qk_dim) + EPS)
        k_ref[:, o:o + MLA_NOPE_DIM] = (k_n * rk * kg[:, :MLA_NOPE_DIM]).astype(BF16)
        k_ref[:, o + MLA_NOPE_DIM:o + MLA_QK_PAD] = (k_pe * rk * kg[:, MLA_NOPE_DIM:]).astype(BF16)
        v_ref[:, h * MLA_V_DIM:(h + 1) * MLA_V_DIM] = kv[:, o + MLA_NOPE_DIM:o + MLA_QK_PAD].astype(BF16)


def _mla_prep(proj, mla_blk, qa_g, kva_g, w_uq, w_ukv, q_g, k_g, cos_t, sina_t, sinb_t, seq):
    T = proj.shape[0]
    tm = _tile(seq, 512)
    ns = seq // tm
    W = MLA_HEADS * MLA_QK_PAD
    const = lambda i: (0, 0)
    pos = lambda i: (i % ns, 0)
    return pl.pallas_call(
        _mla_prep_body,
        out_shape=(jax.ShapeDtypeStruct((T, W), BF16), jax.ShapeDtypeStruct((T, W), BF16),
                   jax.ShapeDtypeStruct((T, MLA_HEADS * MLA_V_DIM), BF16)),
        grid=(T // tm,),
        in_specs=[pl.BlockSpec((tm, MLA_IN_W), lambda i: (i, mla_blk)),
                  pl.BlockSpec((1, MLA_Q_RANK), const),
                  pl.BlockSpec((1, MLA_KV_RANK), const),
                  pl.BlockSpec((MLA_Q_RANK, W), const),
                  pl.BlockSpec((MLA_KV_RANK, W), const),
                  pl.BlockSpec((1, MLA_QK_PAD), const),
                  pl.BlockSpec((1, MLA_QK_PAD), const),
                  pl.BlockSpec((tm, LANES), pos),
                  pl.BlockSpec((tm, LANES), pos),
                  pl.BlockSpec((tm, LANES), pos)],
        out_specs=(pl.BlockSpec((tm, W), lambda i: (i, 0)),
                   pl.BlockSpec((tm, W), lambda i: (i, 0)),
                   pl.BlockSpec((tm, MLA_HEADS * MLA_V_DIM), lambda i: (i, 0))),
        compiler_params=_cparams(("parallel",)),
        name="mla_prep",
    )(proj, qa_g, kva_g, w_uq, w_ukv, q_g, k_g, cos_t, sina_t, sinb_t)


def _mla_attn_body(qg_ref, kg_ref, q_ref, k_ref, v_ref, o_ref, *scratch, tq, n_heads):
    qi = pl.program_id(2)
    heads = range(n_heads)
    state = [scratch[3 * h:3 * h + 3] for h in heads]
    for m, l, acc in state:
        _softmax_init(m, l, acc)
    qs = [q_ref[:, h * MLA_QK_PAD:(h + 1) * MLA_QK_PAD] for h in heads]

    def block_rows(j):
        return pl.ds(pl.multiple_of(j * tq, tq), tq)

    def scores(j):
        return [lax.dot_general(qs[h], k_ref[block_rows(j), h * MLA_QK_PAD:(h + 1) * MLA_QK_PAD], _NT,
                                preferred_element_type=F32) for h in heads]

    causal = (lax.broadcasted_iota(I32, (tq, tq), 1) <= lax.broadcasted_iota(I32, (tq, tq), 0))

    def sweep_all_blocks(softmax_step):
        def update(j, s, masked=False):
            for h in heads:
                m, l, acc = state[h]
                s_h = jnp.where(causal, s[h], NEG_BIG) if masked else s[h]
                softmax_step(s_h, m, l, acc, v_ref[block_rows(j), h * MLA_V_DIM:(h + 1) * MLA_V_DIM])

        _sweep_far_blocks(jnp.maximum(qi - 1, 0), scores, update)

        @pl.when(qi >= 1)
        def _():
            s_prev = scores(qi - 1)
            s_diag = scores(qi)
            update(qi - 1, s_prev)
            update(qi, s_diag, masked=True)

        @pl.when(qi == 0)
        def _():
            update(0, scores(0), masked=True)

    qk_dim = MLA_NOPE_DIM + MLA_ROPE_DIM
    bound = _qk_norm_score_bound(qk_dim, qk_dim ** -0.5 * LOG2_E, qg_ref[...], kg_ref[...])
    bounded = 2.0 * bound <= BOUNDED_SOFTMAX_RANGE

    @pl.when(bounded)
    def _():
        sweep_all_blocks(lambda s, m_, l_, acc_, v: _softmax_step_bounded(s, bound, l_, acc_, v))

    @pl.when(jnp.logical_not(bounded))
    def _():
        sweep_all_blocks(_softmax_step)
    for h in heads:
        _, l, acc = state[h]
        o_ref[:, h * MLA_V_DIM:(h + 1) * MLA_V_DIM] = _softmax_result(l, acc).astype(BF16)


def _mla_attention(q, k, v, q_gain, k_gain, batch, seq, tq):
    T = q.shape[0]
    nq = seq // tq
    hp = 2
    gain_spec = pl.BlockSpec((1, MLA_QK_PAD), lambda b, h, i: (0, 0))
    return pl.pallas_call(
        functools.partial(_mla_attn_body, tq=tq, n_heads=hp),
        out_shape=jax.ShapeDtypeStruct((T, BRANCH_WIDTH), BF16),
        grid=(batch, MLA_HEADS // hp, nq),
        in_specs=[gain_spec, gain_spec,
                  pl.BlockSpec((tq, hp * MLA_QK_PAD), lambda b, h, i: (b * nq + i, h)),
                  pl.BlockSpec((seq, hp * MLA_QK_PAD), lambda b, h, i: (b, h)),
                  pl.BlockSpec((seq, hp * MLA_V_DIM), lambda b, h, i: (b, h))],
        out_specs=pl.BlockSpec((tq, hp * MLA_V_DIM), lambda b, h, i: (b * nq + i, h)),
        scratch_shapes=[pltpu.VMEM((tq, LANES), F32), pltpu.VMEM((tq, LANES), F32),
                        pltpu.VMEM((tq, MLA_V_DIM), F32)] * hp,
        compiler_params=_cparams(("parallel", "parallel", "arbitrary")),
        name="mla_attn",
    )(q_gain, k_gain, q, k, v)


def _merge_body(oa_ref, ob_ref, oc_ref, wb_ref, ga_ref, gb_ref, gc_ref, bg_ref, o_ref):
    acc = None
    for b, (o_r, g_r) in enumerate(((oa_ref, ga_ref), (ob_ref, gb_ref), (oc_ref, gc_ref))):
        gate = jax.nn.sigmoid(g_r[...].astype(F32) + bg_ref[b:b + 1, :])
        t = gate * jnp.dot(o_r[...], wb_ref[0, b], preferred_element_type=F32)
        acc = t if acc is None else acc + t
    o_ref[...] = acc.astype(BF16)


def _branch_merge(o_a, o_b, o_c, w_branch_all, layer, proj, b_gate, d_model):
    T = o_a.shape[0]
    tm, tn = _tile(T, 1024), _tile(d_model, 512)
    g0 = (QKV_W + MLA_IN_W) // tn
    nj = d_model // tn
    o_spec = pl.BlockSpec((tm, BRANCH_WIDTH), lambda i, j: (i, 0))
    gate_spec = lambda b: pl.BlockSpec((tm, tn), lambda i, j: (i, g0 + b * nj + j))
    return pl.pallas_call(
        _merge_body,
        out_shape=jax.ShapeDtypeStruct((T, d_model), BF16),
        grid=(T // tm, nj),
        in_specs=[o_spec, o_spec, o_spec,
                  pl.BlockSpec((1, N_BRANCH, BRANCH_WIDTH, tn), lambda i, j: (layer, 0, 0, j)),
                  gate_spec(0), gate_spec(1), gate_spec(2),
                  pl.BlockSpec((N_BRANCH, tn), lambda i, j: (0, j))],
        out_specs=pl.BlockSpec((tm, tn), lambda i, j: (i, j)),
        compiler_params=_cparams(("parallel", "arbitrary")),
        name="branch_merge",
    )(o_a, o_b, o_c, w_branch_all, proj, proj, proj, b_gate)


def _out_proj_body(a_ref, w_ref, r_ref, o_ref):
    o_ref[...] = r_ref[...] + jnp.dot(a_ref[...], w_ref[0], preferred_element_type=F32)


def _out_proj(merged, w_out_all, layer, x):
    T, K = merged.shape
    N = w_out_all.shape[2]
    tm, tn = _tile(T, 1024), _tile(N, 1024)
    return pl.pallas_call(
        _out_proj_body,
        out_shape=jax.ShapeDtypeStruct((T, N), F32),
        grid=(T // tm, N // tn),
        in_specs=[pl.BlockSpec((tm, K), lambda i, j: (i, 0)),
                  pl.BlockSpec((1, K, tn), lambda i, j: (layer, 0, j)),
                  pl.BlockSpec((tm, tn), lambda i, j: (i, j))],
        out_specs=pl.BlockSpec((tm, tn), lambda i, j: (i, j)),
        compiler_params=_cparams(("parallel", "arbitrary")),
        name="out_proj",
    )(merged, w_out_all, x)


def _route_body(x_ref, g_ref, wr_ref, br_ref, comb_ref, meta_ref, cnt_ref, carry_s, *, tm):
    i = pl.program_id(0)

    @pl.when(i == 0)
    def _():
        carry_s[...] = jnp.zeros(carry_s.shape, F32)

    x = x_ref[...]
    h = x * lax.rsqrt(jnp.mean(x * x, axis=-1, keepdims=True) + EPS) * g_ref[...]
    logits = jnp.dot(h, wr_ref[...], preferred_element_type=F32,
                     precision=lax.Precision.HIGHEST) + br_ref[...]
    lane = lax.broadcasted_iota(I32, (tm, LANES), 1)
    big = jnp.int32(LANES)

    gl = jnp.where(lane < N_GROUPS, logits, -jnp.inf)
    g_max = jnp.max(gl, axis=-1, keepdims=True)
    g_idx = jnp.min(jnp.where(gl == g_max, lane, big), axis=-1, keepdims=True)
    g_val = 1.0 / jnp.sum(jnp.exp(gl - g_max), axis=-1, keepdims=True)

    e_lo = N_GROUPS + g_idx * EXPERTS_PER_GROUP
    el = jnp.where(jnp.logical_and(lane >= e_lo, lane < e_lo + EXPERTS_PER_GROUP), logits, -jnp.inf)
    e1 = jnp.max(el, axis=-1, keepdims=True)
    i1 = jnp.min(jnp.where(el == e1, lane, big), axis=-1, keepdims=True)
    el2 = jnp.where(lane == i1, -jnp.inf, el)
    e2 = jnp.max(el2, axis=-1, keepdims=True)
    i2 = jnp.min(jnp.where(el2 == e2, lane, big), axis=-1, keepdims=True)
    t = jnp.exp(e2 - e1)
    w1 = g_val / (1.0 + t)
    w2 = g_val * t / (1.0 + t)
    comb_ref[...] = jnp.where(lane == i1, w1, 0.0) + jnp.where(lane == i2, w2, 0.0)

    onehot = jnp.where(lane == g_idx, 1.0, 0.0)
    r = lax.broadcasted_iota(I32, (tm, tm), 0)
    c = lax.broadcasted_iota(I32, (tm, tm), 1)
    before = jnp.where(c < r, 1.0, 0.0).astype(BF16)
    prefix = jnp.dot(before, onehot.astype(BF16), preferred_element_type=F32) + carry_s[...]
    rank = jnp.sum(jnp.where(lane == g_idx, prefix, 0.0), axis=-1, keepdims=True)
    carry_s[...] += jnp.sum(onehot, axis=0, keepdims=True)
    meta_ref[...] = jnp.where(lane == 0, g_idx, jnp.where(lane == 1, rank.astype(I32), 0))
    cnt_ref[...] = jnp.broadcast_to(carry_s[...], cnt_ref.shape).astype(I32)


def _moe_route(x, g, w_r, b_r):
    T, D = x.shape
    tm = _tile(T, 512)
    return pl.pallas_call(
        functools.partial(_route_body, tm=tm),
        out_shape=(jax.ShapeDtypeStruct((T, LANES), F32),
                   jax.ShapeDtypeStruct((T, LANES), I32), jax.ShapeDtypeStruct((8, LANES), I32)),
        grid=(T // tm,),
        in_specs=[pl.BlockSpec((tm, D), lambda i: (i, 0)),
                  pl.BlockSpec((1, D), lambda i: (0, 0)),
                  pl.BlockSpec((D, LANES), lambda i: (0, 0)),
                  pl.BlockSpec((1, LANES), lambda i: (0, 0))],
        out_specs=(pl.BlockSpec((tm, LANES), lambda i: (i, 0)),
                   pl.BlockSpec((tm, LANES), lambda i: (i, 0)),
                   pl.BlockSpec((8, LANES), lambda i: (0, 0))),
        scratch_shapes=[pltpu.VMEM((1, LANES), F32)],
        compiler_params=_cparams(("arbitrary",)),
        name="moe_route",
    )(x, g.reshape(1, D), w_r, b_r)


def _start_and_wait_rows(row_copy, n_rows):
    def issue(r, carry):
        row_copy(r).start()
        return carry
    lax.fori_loop(0, n_rows, issue, 0, unroll=ROW_DMA_UNROLL)

    def drain(r, carry):
        row_copy(r).wait()
        return carry
    lax.fori_loop(0, n_rows, drain, 0, unroll=ROW_DMA_UNROLL)


def _dispatch_body(dest_ref, x_ref, g_ref, comb_ref, xs_in, xs_ref, row_buf, sem, *, tm):
    del xs_in
    d_model = x_ref.shape[1]
    x = x_ref[...]
    row_buf[:, :d_model] = x * lax.rsqrt(jnp.mean(x * x, axis=-1, keepdims=True) + EPS) * g_ref[...]
    row_buf[:, d_model:] = comb_ref[...]
    base = pl.program_id(0) * tm

    def row_copy(r):
        return pltpu.make_async_copy(row_buf.at[pl.ds(r, 1)], xs_ref.at[pl.ds(dest_ref[base + r], 1)], sem.at[0])
    _start_and_wait_rows(row_copy, tm)


def _moe_dispatch(dest, x, g, comb, t_pad):
    T, D = x.shape
    W = D + LANES
    tm = _tile(T, ROW_DMA_TILE)
    return pl.pallas_call(
        functools.partial(_dispatch_body, tm=tm),
        out_shape=jax.ShapeDtypeStruct((t_pad, W), F32),
        grid_spec=pltpu.PrefetchScalarGridSpec(
            num_scalar_prefetch=1, grid=(T // tm,),
            in_specs=[pl.BlockSpec((tm, D), lambda i, d: (i, 0)),
                      pl.BlockSpec((1, D), lambda i, d: (0, 0)),
                      pl.BlockSpec((tm, LANES), lambda i, d: (i, 0)),
                      pl.BlockSpec(memory_space=pl.ANY)],
            out_specs=pl.BlockSpec(memory_space=pl.ANY),
            scratch_shapes=[pltpu.VMEM((tm, W), F32), pltpu.SemaphoreType.DMA((1,))]),
        input_output_aliases={4: 0},
        compiler_params=_cparams(("arbitrary",)),
        name="moe_dispatch",
    )(dest, x, g.reshape(1, D), comb, jnp.zeros((t_pad, W), F32))


def _gather_rows_into(idx_ref, table_ref, dst_ref, sem, n_rows):
    base = pl.program_id(0) * n_rows

    def row_copy(r):
        return pltpu.make_async_copy(table_ref.at[pl.ds(idx_ref[base + r], 1)], dst_ref.at[pl.ds(r, 1)], sem.at[0])
    _start_and_wait_rows(row_copy, n_rows)


def _experts_body(tg_ref, nt_ref, xs_ref, wgu_ref, wd_ref, y_ref, *, d_expert):
    i, e = pl.program_id(0), pl.program_id(1)
    valid = i < nt_ref[0]
    d_model = y_ref.shape[1]

    @pl.when(valid)
    def _():
        x = xs_ref[:, :d_model].astype(BF16)
        gu = jnp.dot(x, wgu_ref[0], preferred_element_type=F32)
        comb = xs_ref[:, d_model:]
        lane = lax.broadcasted_iota(I32, comb.shape, 1)
        sel = N_GROUPS + tg_ref[i] * EXPERTS_PER_GROUP + e
        c = jnp.sum(jnp.where(lane == sel, comb, 0.0), axis=-1, keepdims=True)
        a = (jax.nn.silu(gu[:, :d_expert]) * gu[:, d_expert:] * c).astype(BF16)
        y = jnp.dot(a, wd_ref[0].astype(BF16), preferred_element_type=F32)

        @pl.when(e == 0)
        def _():
            y_ref[...] = y

        @pl.when(e > 0)
        def _():
            y_ref[...] += y

    @pl.when(jnp.logical_and(jnp.logical_not(valid), e == 0))
    def _():
        y_ref[...] = jnp.zeros(y_ref.shape, F32)


def _moe_experts(tile_group, n_tiles, xs, w_gu, w_down, layer, tg):
    t_pad = xs.shape[0]
    d_expert, D = w_down.shape[1:]
    expert = lambda i, e, tgr, ntr: (layer * N_EXPERTS + tgr[i] * EXPERTS_PER_GROUP + e, 0, 0)
    rows = lambda i, e, tgr, ntr: (i, 0)
    return pl.pallas_call(
        functools.partial(_experts_body, d_expert=d_expert),
        out_shape=jax.ShapeDtypeStruct((t_pad, D), F32),
        grid_spec=pltpu.PrefetchScalarGridSpec(
            num_scalar_prefetch=2, grid=(t_pad // tg, EXPERTS_PER_GROUP),
            in_specs=[pl.BlockSpec((tg, D + LANES), rows),
                      pl.BlockSpec((1, D, 2 * d_expert), expert),
                      pl.BlockSpec((1, d_expert, D), expert)],
            out_specs=pl.BlockSpec((tg, D), rows)),
        compiler_params=_cparams(("parallel", "arbitrary")),
        name="moe_experts",
    )(tile_group, n_tiles, xs, w_gu, w_down)


def _combine_body(dest_ref, x_ref, ys_ref, o_ref, buf, sem, *, tm):
    _gather_rows_into(dest_ref, ys_ref, buf, sem, tm)
    o_ref[...] = x_ref[...] + buf[...]


def _moe_combine(dest, x, ys):
    T, D = x.shape
    tm = _tile(T, ROW_DMA_TILE)
    return pl.pallas_call(
        functools.partial(_combine_body, tm=tm),
        out_shape=jax.ShapeDtypeStruct((T, D), F32),
        grid_spec=pltpu.PrefetchScalarGridSpec(
            num_scalar_prefetch=1, grid=(T // tm,),
            in_specs=[pl.BlockSpec((tm, D), lambda i, d: (i, 0)),
                      pl.BlockSpec(memory_space=pl.ANY)],
            out_specs=pl.BlockSpec((tm, D), lambda i, d: (i, 0)),
            scratch_shapes=[pltpu.VMEM((tm, D), F32), pltpu.SemaphoreType.DMA((1,))]),
        compiler_params=_cparams(("arbitrary",)),
        name="moe_combine",
    )(dest, x, ys)


def _hier_moe(x, ln_g, w_r, b_r, w_gu, w_down, layer):
    T, D = x.shape
    tg = _tile(T, 512)
    comb, meta, counts = _moe_route(x, ln_g, w_r, b_r)
    cnt = counts[0, :N_GROUPS]
    seg_tiles = (cnt + tg - 1) // tg
    tile_end = jnp.cumsum(seg_tiles)
    seg_off = (tile_end - seg_tiles) * tg
    dest = seg_off[meta[:, 0]] + meta[:, 1]
    n_tiles_max = T // tg + N_GROUPS
    n_tiles = tile_end[-1]
    tile_ids = jnp.arange(n_tiles_max, dtype=I32)
    tile_group = jnp.searchsorted(tile_end, jnp.minimum(tile_ids, n_tiles - 1), side="right").astype(I32)
    dest = dest.astype(I32)
    xs = _moe_dispatch(dest, x, ln_g, comb, n_tiles_max * tg)
    ys = _moe_experts(tile_group, n_tiles.reshape(1).astype(I32), xs, w_gu, w_down, layer, tg)
    return _moe_combine(dest, x, ys)


def _layout_w_in(w_in):
    mla_w = MLA_Q_RANK + MLA_KV_RANK + MLA_ROPE_DIM
    w_a = jnp.pad(w_in[:, :, :QKV_W + mla_w].astype(BF16), ((0, 0), (0, 0), (0, MLA_IN_W - mla_w)))
    w_b = w_in[:, :, QKV_W + mla_w:].astype(BF16)
    return w_a, w_b


def _layout_w_uq(w_uq):
    w = w_uq.reshape(MLA_Q_RANK, MLA_HEADS, MLA_NOPE_DIM + MLA_ROPE_DIM)
    w = jnp.pad(w, ((0, 0), (0, 0), (0, MLA_QK_PAD - MLA_NOPE_DIM - MLA_ROPE_DIM)))
    return w.reshape(MLA_Q_RANK, MLA_HEADS * MLA_QK_PAD).astype(BF16)


def _pad_gain(g):
    return jnp.pad(g, (0, MLA_QK_PAD - g.shape[0])).reshape(1, MLA_QK_PAD)


def _rope_tables(seq):
    pos = jnp.arange(seq, dtype=F32)
    inv_freq = 1.0 / (ROPE_THETA ** (jnp.arange(0, MLA_ROPE_DIM, 2, dtype=F32) / MLA_ROPE_DIM))
    ang = pos[:, None] * inv_freq[None, :]
    cos, sin = jnp.cos(ang), jnp.sin(ang)
    z = jnp.zeros_like(cos)
    return (jnp.concatenate([cos, cos, z, z], axis=1),
            jnp.concatenate([-sin, z, z, z], axis=1),
            jnp.concatenate([z, sin, z, z], axis=1))


def kernel(x, rel_bias, ln1_g, w_in, diff_q_norm, diff_k_norm, diff_lambda, diff_subln_g, mla_q_a_norm, mla_w_uq, mla_kv_a_norm, mla_w_ukv, mla_q_norm, mla_k_norm, w_branch, b_gate, w_out, ln2_g, router_group_w, router_group_b, router_expert_w, router_expert_b, expert_w_gu, expert_w_down):
    B, S, D = x.shape
    depth = w_in.shape[0]
    T = B * S
    tq = _tile(S, 512)
    tq_sb = _tile(S, 256)
    cos_t, sina_t, sinb_t = _rope_tables(S)
    bias_tiles, bias_max = _t5_bias_tiles(rel_bias, tq)
    w_in_a, w_in_b = _layout_w_in(w_in)
    mla_blk = QKV_W // MLA_IN_W
    w_branch_all = w_branch.astype(BF16)
    w_out_all = w_out.astype(BF16)
    w_gu_all = expert_w_gu.astype(BF16).reshape((depth * N_EXPERTS,) + expert_w_gu.shape[2:])
    w_down_all = expert_w_down.reshape((depth * N_EXPERTS,) + expert_w_down.shape[2:])
    xt = x.reshape(T, D)
    for l in range(depth):
        proj = _rms_matmul(xt, ln1_g[l], w_in_a, w_in_b, l, BF16, "rms_in_proj")

        lam_init = 0.8 - 0.6 * math.exp(-0.3 * l)
        lam_tab = jnp.concatenate([diff_lambda[l], jnp.full((4, DIFF_QK_DIM), lam_init, F32)], axis=0)
        o_a = _diff_attention(proj, lam_tab,
                              jnp.tile(diff_q_norm[l], 2).reshape(1, HEAD_W),
                              jnp.tile(diff_k_norm[l], 2).reshape(1, HEAD_W),
                              diff_subln_g[l].reshape(1, HEAD_W), bias_tiles, bias_max, B, S, tq)
        o_b = _sb_attention(proj, B, S, tq_sb)
        mla_qg, mla_kg = _pad_gain(mla_q_norm[l]), _pad_gain(mla_k_norm[l])
        q_c, k_c, v_c = _mla_prep(proj, mla_blk, mla_q_a_norm[l].reshape(1, -1), mla_kv_a_norm[l].reshape(1, -1),
                                  _layout_w_uq(mla_w_uq[l]), mla_w_ukv[l].astype(BF16),
                                  mla_qg, mla_kg, cos_t, sina_t, sinb_t, S)
        o_c = _mla_attention(q_c, k_c, v_c, mla_qg, mla_kg, B, S, tq)

        merged = _branch_merge(o_a, o_b, o_c, w_branch_all, l, proj, b_gate[l].reshape(N_BRANCH, D), D)
        xt = _out_proj(merged, w_out_all, l, xt)

        w_r = jnp.concatenate([router_group_w[l], router_expert_w[l],
                               jnp.zeros((D, LANES - N_GROUPS - N_EXPERTS), F32)], axis=1)
        b_r = jnp.concatenate([router_group_b[l], router_expert_b[l],
                               jnp.zeros((LANES - N_GROUPS - N_EXPERTS,), F32)]).reshape(1, LANES)
        xt = _hier_moe(xt, ln2_g[l], w_r, b_r, w_gu_all, w_down_all, l)
    return xt.reshape(B, S, D)
```
